```python
import math
import jax, jax.numpy as jnp
from jax import lax
import numpy as np

D_MODEL = 1024
BATCH = 4
SEQ = 4096
DEPTH = 4
DEC_BATCH = 128
DEC_SEQ = 4
PAST_LEN = 2048
PAGE_SIZE = 128

N_MIXERS = 2
N_LAYERS_A = (DEPTH + 1) // 2
N_LAYERS_B = DEPTH // 2

HEADS_A = 8
DK_A = 128
DV_A = 128
QK_WIDTH_A = HEADS_A * DK_A
V_WIDTH_A = HEADS_A * DV_A
CONV_W = 4
CONV_CH = 2 * QK_WIDTH_A + V_WIDTH_A
CHUNK_A = 64
IN_A = CONV_CH + V_WIDTH_A + 2 * HEADS_A

GROUPS_B = ((128, 1), (512, 4), (2048, 16))
N_GROUPS_B = len(GROUPS_B)
HEADS_B = 8
DH_B = 128
WIDTH_B = HEADS_B * DH_B
IN_B = N_GROUPS_B * 3 * WIDTH_B
ROT_DIM = DH_B // 4
ROPE_THETA = 500000.0

N_EXPERTS = 16
N_EXPERT_GROUPS = 4
EXPERTS_PER_GROUP = N_EXPERTS // N_EXPERT_GROUPS
TOP_K = 2
D_EXPERT = 512

NORM_EPS = 1e-6

kernel_name = 'hybrid_deltanet_dilated_swa_moe_adaln_step'


def rmsnorm(x, g):
    xf = x.astype(jnp.float32)
    y = xf * lax.rsqrt(jnp.mean(xf * xf, -1, keepdims=True) + NORM_EPS)
    return (y * g.astype(jnp.float32)).astype(x.dtype)


def l2norm(x):
    xf = x.astype(jnp.float32)
    return xf * lax.rsqrt(jnp.sum(xf * xf, -1, keepdims=True) + NORM_EPS)


def partial_rope(x, pos):
    half = ROT_DIM // 2
    inv = ROPE_THETA ** (-jnp.arange(half, dtype=jnp.float32) / half)
    ang = pos.astype(jnp.float32)[:, None] * inv[None, :]
    cos = jnp.cos(ang)[None, :, None, :]
    sin = jnp.sin(ang)[None, :, None, :]
    xr = x[..., :ROT_DIM].astype(jnp.float32)
    x1, x2 = xr[..., :half], xr[..., half:]
    rot = jnp.concatenate([x1 * cos - x2 * sin, x2 * cos + x1 * sin], -1)
    return jnp.concatenate([rot.astype(x.dtype), x[..., ROT_DIM:]], -1)


def short_conv(x, buf, w):
    T = x.shape[1]
    xp = jnp.concatenate([buf.astype(x.dtype), x], 1)
    y = xp[:, 0:T] * w[0]
    for j in range(1, CONV_W):
        y = y + xp[:, j:j + T] * w[j]
    return jax.nn.silu(y), xp[:, -(CONV_W - 1):]


def gated_delta_chunked(q, k, v, g, beta, S0, chunk):
    B, T, H, _ = q.shape
    DV = v.shape[-1]
    NC = T // chunk

    def blk(a):
        a = a.reshape((B, NC, chunk) + a.shape[2:])
        return jnp.moveaxis(a, 3, 1)

    qc, kc, vc, gc, bc = blk(q), blk(k), blk(v), blk(g), blk(beta)
    G = jnp.cumsum(gc, -1)
    ii = jnp.arange(chunk)
    tril = ii[:, None] >= ii[None, :]
    stril = ii[:, None] > ii[None, :]
    decay = jnp.exp(jnp.where(tril, G[..., :, None] - G[..., None, :], -jnp.inf))
    kb = kc * bc[..., None]
    A = jnp.where(stril, jnp.einsum('bhncd,bhnsd->bhncs', kb, kc) * decay, 0.0)
    eye = jnp.eye(chunk, dtype=A.dtype)
    rhs = jnp.concatenate([vc * bc[..., None], kb * jnp.exp(G)[..., None]], -1)
    sol = lax.linalg.triangular_solve(eye + A, rhs, left_side=True, lower=True, unit_diagonal=True)
    u, w = sol[..., :DV], sol[..., DV:]
    Aqk = jnp.where(tril, jnp.einsum('bhncd,bhnsd->bhncs', qc, kc) * decay, 0.0)
    q_dec = qc * jnp.exp(G)[..., None]
    glast = G[..., -1]
    k_dec = kc * jnp.exp(glast[..., None] - G)[..., None]
    xs = tuple(jnp.moveaxis(a, 2, 0) for a in (u, w, Aqk, q_dec, k_dec, glast))

    def step(S, inp):
        u_n, w_n, aqk_n, qd_n, kd_n, gl_n = inp
        v_new = u_n - jnp.einsum('bhcd,bhde->bhce', w_n, S)
        o = jnp.einsum('bhcd,bhde->bhce', qd_n, S) + jnp.einsum('bhcs,bhse->bhce', aqk_n, v_new)
        S = S * jnp.exp(gl_n)[..., None, None] + jnp.einsum('bhcd,bhce->bhde', kd_n, v_new)
        return S, o

    S, o = lax.scan(step, S0, xs)
    o = jnp.transpose(o, (1, 0, 3, 2, 4)).reshape(B, T, H, DV)
    return o, S


def delta_mixer(h, conv_buf, S0, w_in, conv_w, a_log, dt_bias, o_gain, w_out):
    B, T, _ = h.shape
    p = h @ w_in
    qkv, z, a, b = jnp.split(p, [CONV_CH, CONV_CH + V_WIDTH_A, CONV_CH + V_WIDTH_A + HEADS_A], -1)
    qkv, conv_state = short_conv(qkv, conv_buf, conv_w)
    q, k, v = jnp.split(qkv, [QK_WIDTH_A, 2 * QK_WIDTH_A], -1)
    q = l2norm(q.reshape(B, T, HEADS_A, DK_A)) * (DK_A ** -0.5)
    k = l2norm(k.reshape(B, T, HEADS_A, DK_A))
    v = v.reshape(B, T, HEADS_A, DV_A).astype(jnp.float32)
    beta = jax.nn.sigmoid(b.astype(jnp.float32))
    g = -jnp.exp(a_log.astype(jnp.float32)) * jax.nn.softplus(a.astype(jnp.float32) + dt_bias.astype(jnp.float32))
    chunk = CHUNK_A if T % CHUNK_A == 0 else T
    o, S = gated_delta_chunked(q, k, v, g, beta, S0.astype(jnp.float32), chunk)
    o = rmsnorm(o, o_gain) * jax.nn.silu(z.reshape(B, T, HEADS_A, DV_A).astype(jnp.float32))
    y = o.reshape(B, T, V_WIDTH_A).astype(h.dtype) @ w_out
    return y, conv_state, S.astype(h.dtype)


def dilated_group_prompt(q, k, v, win, dil):
    B, T, H, E = q.shape
    nw = win // dil
    L = T // dil
    nb = -(-L // nw)
    Lp = nb * nw

    def sub(a, front):
        a = a.reshape(B, L, dil, H, E)
        return jnp.pad(a, ((0, 0), (front, Lp - L), (0, 0), (0, 0), (0, 0)))

    qb = sub(q, 0).reshape(B, nb, nw, dil, H, E)
    kp = sub(k, nw).reshape(B, nb + 1, nw, dil, H, E)
    vp = sub(v, nw).reshape(B, nb + 1, nw, dil, H, E)
    kb = jnp.concatenate([kp[:, :-1], kp[:, 1:]], axis=2)
    vb = jnp.concatenate([vp[:, :-1], vp[:, 1:]], axis=2)
    s = jnp.einsum('bnqrhe,bnkrhe->bnrhqk', qb, kb, preferred_element_type=jnp.float32) * (E ** -0.5)
    qi = jnp.arange(nw)[:, None]
    kj = jnp.arange(2 * nw)[None, :]
    dist = qi + nw - kj
    kidx = jnp.arange(nb)[:, None, None] * nw + kj[None] - nw
    valid = (dist >= 0)[None] & (dist <= nw)[None] & (kidx >= 0)
    s = jnp.where(valid[None, :, None, None], s, -jnp.inf)
    m = jnp.max(s, -1, keepdims=True)
    p = jnp.exp(s - m)
    den = jnp.sum(p, -1, keepdims=True)
    o = jnp.einsum('bnrhqk,bnkrhe->bnqrhe', p / den, vb, preferred_element_type=jnp.float32)
    lse = jnp.transpose((m + jnp.log(den))[..., 0], (0, 1, 4, 2, 3))
    o = o.reshape(B, Lp, dil, H, E)[:, :L].reshape(B, T, H, E)
    lse = lse.reshape(B, Lp, dil, H)[:, :L].reshape(B, T, H)
    return o, lse


def dilated_group_sample(q, k, v, buf, win, dil):
    S, E = q.shape[1], q.shape[-1]
    wb = buf.shape[1]
    kf = jnp.concatenate([buf[:, :, 0].astype(k.dtype), k], 1)
    vf = jnp.concatenate([buf[:, :, 1].astype(v.dtype), v], 1)
    idx = wb + jnp.arange(S)[:, None] - dil * jnp.arange(win // dil + 1)[None, :]
    valid = idx >= 0
    idx = jnp.maximum(idx, 0)
    kg = kf[:, idx]
    vg = vf[:, idx]
    s = jnp.einsum('bshe,bsjhe->bhsj', q, kg, preferred_element_type=jnp.float32) * (E ** -0.5)
    s = jnp.where(valid[None, None], s, -jnp.inf)
    m = jnp.max(s, -1, keepdims=True)
    p = jnp.exp(s - m)
    den = jnp.sum(p, -1, keepdims=True)
    o = jnp.einsum('bhsj,bsjhe->bshe', p / den, vg, preferred_element_type=jnp.float32)
    lse = jnp.transpose((m + jnp.log(den))[..., 0], (0, 2, 1))
    return o, lse


def merge_groups(outs, lses):
    w = jax.nn.softmax(jnp.stack(lses, -1), -1)
    return jnp.einsum('bthg,bthge->bthe', w, jnp.stack(outs, 3))


def dilated_mixer(h, w_in, w_out, bufs):
    B, T, _ = h.shape
    t0 = 0 if bufs is None else PAST_LEN
    pos = t0 + jnp.arange(T, dtype=jnp.int32)
    p = (h @ w_in).reshape(B, T, N_GROUPS_B, 3, HEADS_B, DH_B)
    outs, lses, new = [], [], []
    for gi, (win, dil) in enumerate(GROUPS_B):
        q = partial_rope(p[:, :, gi, 0], pos)
        k = partial_rope(p[:, :, gi, 1], pos)
        v = p[:, :, gi, 2]
        if bufs is None:
            o, lse = dilated_group_prompt(q, k, v, win, dil)
            keep = min(win, T)
            new.append(jnp.stack([k[:, T - keep:], v[:, T - keep:]], 2))
        else:
            o, lse = dilated_group_sample(q, k, v, bufs[gi], win, dil)
            new.append(jnp.stack([k, v], 2))
        outs.append(o)
        lses.append(lse)
    y = merge_groups(outs, lses).reshape(B, T, WIDTH_B).astype(h.dtype) @ w_out
    return y, new


def moe(h, router_w, router_b, w_gate, w_up, w_down):
    B, T, _ = h.shape
    logits = jnp.einsum('btd,de->bte', h, router_w).astype(jnp.float32) + router_b.astype(jnp.float32)
    probs = jax.nn.softmax(logits, -1)
    pg = probs.reshape(B, T, N_EXPERT_GROUPS, EXPERTS_PER_GROUP)
    gscore = jnp.sum(lax.top_k(pg, TOP_K)[0], -1)
    gsel = jnp.argmax(gscore, -1)
    gmask = gsel[..., None] == jnp.arange(N_EXPERT_GROUPS)
    masked = jnp.where(gmask[..., None], pg, -1.0).reshape(B, T, N_EXPERTS)
    topv, topi = lax.top_k(masked, TOP_K)
    wts = topv / jnp.sum(topv, -1, keepdims=True)
    gate = jnp.sum(jax.nn.one_hot(topi, N_EXPERTS, dtype=jnp.float32) * wts[..., None], -2)
    hg = jnp.einsum('btd,edf->btef', h, w_gate)
    hu = jnp.einsum('btd,edf->btef', h, w_up)
    a = jax.nn.silu(hg) * hu * gate[..., None].astype(h.dtype)
    return jnp.einsum('btef,efd->btd', a, w_down)


def setup_inputs(seed: int = 0) -> dict:
    key = jax.random.key(seed)
    ks = jax.random.split(key, 32)
    f32 = jnp.float32
    D = D_MODEL

    def nrm(k, shape, s=1.0):
        return jax.random.normal(k, shape, f32) * s

    wb = [min(w, PAST_LEN) for w, _ in GROUPS_B]
    dt = jnp.exp(jax.random.uniform(ks[17], (N_LAYERS_A, HEADS_A), f32, math.log(1e-3), math.log(1e-1)))
    return {
        'x_prompt': nrm(ks[0], (BATCH, SEQ, D)),
        'x_sample': nrm(ks[1], (DEC_BATCH, DEC_SEQ, D)),
        'state_conv': nrm(ks[2], (N_LAYERS_A, DEC_BATCH, CONV_W - 1, CONV_CH)),
        'state_delta': nrm(ks[3], (N_LAYERS_A, DEC_BATCH, HEADS_A, DK_A, DV_A), DK_A ** -0.5),
        'cache_kv0': nrm(ks[4], (N_LAYERS_B, DEC_BATCH, wb[0], 2, HEADS_B, DH_B)),
        'cache_kv1': nrm(ks[5], (N_LAYERS_B, DEC_BATCH, wb[1], 2, HEADS_B, DH_B)),
        'cache_kv2': nrm(ks[6], (N_LAYERS_B, DEC_BATCH, wb[2], 2, HEADS_B, DH_B)),
        'c_prompt': nrm(ks[7], (BATCH, D)),
        'c_sample': nrm(ks[8], (DEC_BATCH, D)),
        'w_ada': nrm(ks[9], (DEPTH, D, 6 * D), 0.5 * D ** -0.5),
        'b_ada': nrm(ks[10], (DEPTH, 6 * D), 0.01),
        'norm_mix': 1.0 + nrm(ks[11], (DEPTH, D), 0.01),
        'norm_ffn': 1.0 + nrm(ks[12], (DEPTH, D), 0.01),
        'norm_final': 1.0 + nrm(ks[13], (D,), 0.01),
        'a_w_in': nrm(ks[14], (N_LAYERS_A, D, IN_A), D ** -0.5),
        'a_conv': nrm(ks[15], (N_LAYERS_A, CONV_W, CONV_CH), CONV_W ** -0.5),
        'a_log': jnp.log(jax.random.uniform(ks[16], (N_LAYERS_A, HEADS_A), f32, 1.0, 16.0)),
        'a_dt_bias': jnp.log(jnp.expm1(dt)),
        'a_out_norm': 1.0 + nrm(ks[18], (N_LAYERS_A, DV_A), 0.01),
        'a_w_out': nrm(ks[19], (N_LAYERS_A, V_WIDTH_A, D), V_WIDTH_A ** -0.5),
        'b_w_in': nrm(ks[20], (N_LAYERS_B, D, IN_B), D ** -0.5),
        'b_w_out': nrm(ks[21], (N_LAYERS_B, WIDTH_B, D), WIDTH_B ** -0.5),
        'router_w': nrm(ks[22], (D, N_EXPERTS), D ** -0.5),
        'router_b': nrm(ks[23], (N_EXPERTS,), 0.01),
        'exp_w_gate': nrm(ks[24], (DEPTH, N_EXPERTS, D, D_EXPERT), D ** -0.5),
        'exp_w_up': nrm(ks[25], (DEPTH, N_EXPERTS, D, D_EXPERT), D ** -0.5),
        'exp_w_down': nrm(ks[26], (DEPTH, N_EXPERTS, D_EXPERT, D), D_EXPERT ** -0.5),
    }


def reference(x_prompt, x_sample, state_conv, state_delta, cache_kv0, cache_kv1, cache_kv2, c_prompt, c_sample,
              w_ada, b_ada, norm_mix, norm_ffn, norm_final, a_w_in, a_conv, a_log, a_dt_bias, a_out_norm, a_w_out,
              b_w_in, b_w_out, router_w, router_b, exp_w_gate, exp_w_up, exp_w_down):

    def run(x, c, conv_in, delta_in, kv_in):
        B = x.shape[0]
        cs = jax.nn.silu(c)
        convs, deltas = [], []
        kvs = [[] for _ in GROUPS_B]
        for i in range(DEPTH):
            mod = cs @ w_ada[i] + b_ada[i]
            sh1, sc1, g1, sh2, sc2, g2 = [m[:, None, :] for m in jnp.split(mod, 6, -1)]
            h = rmsnorm(x, norm_mix[i]) * (1 + sc1) + sh1
            j = i // N_MIXERS
            if i % N_MIXERS == 0:
                if kv_in is None:
                    cb = jnp.zeros((B, CONV_W - 1, CONV_CH), x.dtype)
                    S0 = jnp.zeros((B, HEADS_A, DK_A, DV_A), jnp.float32)
                else:
                    cb, S0 = conv_in[j], delta_in[j]
                y, cnew, snew = delta_mixer(h, cb, S0, a_w_in[j], a_conv[j], a_log[j], a_dt_bias[j],
                                            a_out_norm[j], a_w_out[j])
                convs.append(cnew)
                deltas.append(snew)
            else:
                bufs = None if kv_in is None else tuple(kv[j] for kv in kv_in)
                y, new = dilated_mixer(h, b_w_in[j], b_w_out[j], bufs)
                for lst, n in zip(kvs, new):
                    lst.append(n)
            x = x + g1 * y
            h = rmsnorm(x, norm_ffn[i]) * (1 + sc2) + sh2
            x = x + g2 * moe(h, router_w, router_b, exp_w_gate[i], exp_w_up[i], exp_w_down[i])
        return (rmsnorm(x, norm_final), jnp.stack(convs), jnp.stack(deltas),
                jnp.stack(kvs[0]), jnp.stack(kvs[1]), jnp.stack(kvs[2]))

    y_prompt, conv_p, delta_p, kv0_p, kv1_p, kv2_p = run(x_prompt, c_prompt, None, None, None)
    y_sample, conv_s, delta_s, kv0_s, kv1_s, kv2_s = run(x_sample, c_sample, state_conv, state_delta,
                                                         (cache_kv0, cache_kv1, cache_kv2))
    return (y_prompt, y_sample, conv_p, delta_p, kv0_p, kv1_p, kv2_p, conv_s, delta_s, kv0_s, kv1_s, kv2_s)
```

```python
import functools

import jax
import jax.numpy as jnp
from jax import lax
from jax.experimental import pallas as pl
from jax.experimental.pallas import tpu as pltpu

F32 = jnp.float32
BF16 = jnp.bfloat16

N_MIXERS = 2
HEADS = 8
DH = 128
WIDTH = HEADS * DH
CONV_W = 4
CHUNK = 64
GROUPS = ((128, 1), (512, 4), (2048, 16))
KEYS_BACK = 128
ROT_DIM = DH // 4
ROPE_THETA = 500000.0
N_EXPERTS = 16
N_EXPERT_GROUPS = 4
EXPERTS_PER_GROUP = 4
NORM_EPS = 1e-6

LANES = 128
SUBLANES = 8
VMEM_LIMIT_BYTES = 58 * 1024 * 1024

ROW_TILE = 512
MIX_TILE = 128
FFN_TILE = 256
GATHER_TILE = 256

PAIRS = ((0, 1), (0, 2), (0, 3), (1, 2), (1, 3), (2, 3))
VISIT = ((0, 0, 1, 0), (3, 2, 1, 1), (5, 2, 3, 0), (2, 0, 3, 0), (1, 0, 2, 0), (4, 1, 3, 0))
N_CLASSES = N_EXPERT_GROUPS * len(PAIRS)


def _params(n_grid):
    return pltpu.CompilerParams(dimension_semantics=("arbitrary",) * n_grid,
                                vmem_limit_bytes=VMEM_LIMIT_BYTES)


def _first_step(n_grid):
    first = pl.program_id(0) == 0
    for a in range(1, n_grid):
        first = jnp.logical_and(first, pl.program_id(a) == 0)
    return first


def _silu(x):
    return x * jax.nn.sigmoid(x)


def _norm_mod(x, gain, sc, sh):
    ms = jnp.mean(x * x, axis=-1, keepdims=True)
    return (x * lax.rsqrt(ms + NORM_EPS) * gain) * (1.0 + sc) + sh


def _dot(a, b):
    return jnp.dot(a, b, preferred_element_type=F32)


def _dot_nt(a, b):
    return lax.dot_general(a, b, (((1,), (1,)), ((), ())), preferred_element_type=F32)


def _dot_f32(a, b):
    return jnp.dot(a, b, preferred_element_type=F32, precision=lax.Precision.HIGHEST)


def _cast_rows(dst_ref, src_ref, col0=0):
    rows, ncols = src_ref.shape
    step = 128 if rows % 128 == 0 else rows

    def body(c, carry):
        r0 = pl.multiple_of(c * step, step)
        dst_ref[pl.ds(r0, step), col0:col0 + ncols] = src_ref[pl.ds(r0, step), :].astype(BF16)
        return carry

    lax.fori_loop(0, rows // step, body, 0)


def _adaln_kernel(c_ref, w_ref, b_ref, o_ref):
    cs = _silu(c_ref[...]).astype(BF16)
    o_ref[...] = _dot(cs, w_ref[...].astype(BF16)) + b_ref[...]


def _adaln(c_all, w_ada, b_ada):
    rows, d = c_all.shape
    depth, _, six_d = w_ada.shape
    tn = 1024
    return pl.pallas_call(
        _adaln_kernel,
        grid=(depth, six_d // tn),
        in_specs=[pl.BlockSpec((rows, d), lambda l, j: (0, 0)),
                  pl.BlockSpec((None, d, tn), lambda l, j: (l, 0, j)),
                  pl.BlockSpec((None, 1, tn), lambda l, j: (l, 0, j))],
        out_specs=pl.BlockSpec((None, rows, tn), lambda l, j: (l, 0, j)),
        out_shape=jax.ShapeDtypeStruct((depth, rows, six_d), F32),
        compiler_params=_params(2),
        name="adaln",
    )(c_all, w_ada, b_ada.reshape(depth, 1, six_d))


def _inproj_kernel(x_ref, g_ref, sc_ref, sh_ref, w_ref, *rest, n_grid, n_extra, n_out, epilogue):
    extra = rest[:n_extra]
    outs = rest[n_extra:n_extra + n_out]
    wb_ref = rest[n_extra + n_out]

    @pl.when(_first_step(n_grid))
    def _():
        _cast_rows(wb_ref, w_ref)

    h = _norm_mod(x_ref[...], g_ref[...], sc_ref[...], sh_ref[...]).astype(BF16)
    acc = _dot(h, wb_ref[...])
    epilogue(acc, extra, outs)


def _rope_heads(x, cos, sin_hi, sin_lo):
    parts = []
    for h in range(HEADS):
        xh = x[:, h * DH:(h + 1) * DH]
        parts.append(xh * cos + pltpu.roll(xh, DH - ROT_DIM // 2, 1) * sin_hi
                     + pltpu.roll(xh, ROT_DIM // 2, 1) * sin_lo)
    return jnp.concatenate(parts, axis=1)


def _epi_attn(acc, extra, outs, *, out_dtype):
    cos_ref, shi_ref, slo_ref = extra
    q_ref, k_ref, v_ref = outs
    cos, shi, slo = cos_ref[...], shi_ref[...], slo_ref[...]
    q = _rope_heads(acc[:, :WIDTH], cos, shi, slo) * (DH ** -0.5)
    k = _rope_heads(acc[:, WIDTH:2 * WIDTH], cos, shi, slo)
    q_ref[...] = q.astype(out_dtype)
    k_ref[...] = k.astype(out_dtype)
    v_ref[...] = acc[:, 2 * WIDTH:].astype(out_dtype)


def _epi_delta_qkv(acc, extra, outs, *, with_tail):
    outs[0][...] = acc.astype(outs[0].dtype)
    if with_tail:
        rows = acc.shape[0]
        outs[1][...] = acc[rows - SUBLANES:, :]


def _epi_delta_zab(acc, extra, outs):
    outs[0][...] = acc[:, :WIDTH].astype(outs[0].dtype)
    outs[1][...] = acc[:, WIDTH:]


def _rope_tables(pos):
    half = ROT_DIM // 2
    inv = ROPE_THETA ** (-jnp.arange(half, dtype=F32) / half)
    ang = pos.astype(F32)[:, None] * inv[None, :]
    cos, sin = jnp.cos(ang), jnp.sin(ang)
    n = pos.shape[0]
    pad = jnp.zeros((n, DH - ROT_DIM), F32)
    z = jnp.zeros((n, half), F32)
    cos_t = jnp.concatenate([cos, cos, jnp.ones((n, DH - ROT_DIM), F32)], axis=1)
    sin_hi = jnp.concatenate([-sin, z, pad], axis=1)
    sin_lo = jnp.concatenate([z, sin, pad], axis=1)
    return cos_t, sin_hi, sin_lo


def _inproj_attn_prompt(x, gain, mod3, w_in, g, dil, tables):
    b, t, d = x.shape
    ln = t // dil
    lt = min(ROW_TILE, ln)
    x3 = x.reshape(b, ln, dil * d)
    tabs = [tb.reshape(ln, dil * DH) for tb in tables]
    ncols = 3 * WIDTH
    kernel = functools.partial(_inproj_kernel, n_grid=3, n_extra=3, n_out=3,
                               epilogue=functools.partial(_epi_attn, out_dtype=BF16))
    out_sd = jax.ShapeDtypeStruct((b, dil, ln, WIDTH), BF16)
    out_spec = pl.BlockSpec((None, None, lt, WIDTH), lambda bi, r, li: (bi, r, li, 0))
    tab_spec = pl.BlockSpec((lt, DH), lambda bi, r, li: (li, r))
    return pl.pallas_call(
        kernel,
        grid=(b, dil, ln // lt),
        in_specs=[pl.BlockSpec((None, lt, d), lambda bi, r, li: (bi, li, r)),
                  pl.BlockSpec((1, d), lambda bi, r, li: (0, 0)),
                  pl.BlockSpec((None, 1, d), lambda bi, r, li: (bi, 0, 1)),
                  pl.BlockSpec((None, 1, d), lambda bi, r, li: (bi, 0, 0)),
                  pl.BlockSpec((d, ncols), lambda bi, r, li: (0, g)),
                  tab_spec, tab_spec, tab_spec],
        out_specs=[out_spec, out_spec, out_spec],
        out_shape=[out_sd, out_sd, out_sd],
        scratch_shapes=[pltpu.VMEM((d, ncols), BF16)],
        compiler_params=_params(3),
        name=f"inproj_attn_g{g}",
    )(x3, gain, mod3, mod3, w_in, *tabs)


def _inproj_rows(x2, gain, sc_arr, sh_arr, sc_spec, sh_spec, w, wspec, ncols, extra, extra_specs,
                 out_shapes, out_specs, epilogue, tm, name):
    n, d = x2.shape
    kernel = functools.partial(_inproj_kernel, n_grid=1, n_extra=len(extra), n_out=len(out_shapes),
                               epilogue=epilogue)
    return pl.pallas_call(
        kernel,
        grid=(n // tm,),
        in_specs=[pl.BlockSpec((tm, d), lambda i: (i, 0)),
                  pl.BlockSpec((1, d), lambda i: (0, 0)),
                  sc_spec, sh_spec, wspec] + list(extra_specs),
        out_specs=out_specs,
        out_shape=out_shapes,
        scratch_shapes=[pltpu.VMEM((d, ncols), BF16)],
        compiler_params=_params(1),
        name=name,
    )(x2, gain, sc_arr, sh_arr, w, *extra)


def _mod_specs_prompt(tiles_per_seq, d, k_sc, k_sh):
    sc = pl.BlockSpec((None, 1, d), lambda i: (i // tiles_per_seq, 0, k_sc))
    sh = pl.BlockSpec((None, 1, d), lambda i: (i // tiles_per_seq, 0, k_sh))
    return sc, sh


def _mod_specs_rows(tm, d, k_sc, k_sh):
    sc = pl.BlockSpec((tm, d), lambda i: (i, k_sc))
    sh = pl.BlockSpec((tm, d), lambda i: (i, k_sh))
    return sc, sh


def _delta_prep_kernel(x_ref, halo_ref, ab_ref, cw_ref, alog_ref, dtb_ref,
                       u_ref, w_ref, qd_ref, kdt_ref, aqk_ref, egl_ref, ext_ref):
    tm = x_ref.shape[0]
    ti = pl.program_id(1)
    halo = halo_ref[...].astype(F32)
    ext_ref[0:SUBLANES, :] = jnp.where(ti > 0, halo, 0.0)
    ext_ref[SUBLANES:, :] = x_ref[...].astype(F32)
    y = ext_ref[pl.ds(SUBLANES - (CONV_W - 1), tm), :] * cw_ref[0:1, :]
    for j in range(1, CONV_W):
        y = y + ext_ref[pl.ds(SUBLANES - (CONV_W - 1) + j, tm), :] * cw_ref[j:j + 1, :]
    y = _silu(y)

    ab = ab_ref[...]
    g = -jnp.exp(alog_ref[...]) * jax.nn.softplus(ab + dtb_ref[...])
    beta = jax.nn.sigmoid(ab)
    row = lax.broadcasted_iota(jnp.int32, (tm, tm), 0)
    col = lax.broadcasted_iota(jnp.int32, (tm, tm), 1)
    same = (row // CHUNK) == (col // CHUNK)
    cum = _dot_f32(jnp.where(jnp.logical_and(same, col <= row), 1.0, 0.0), g)
    tot = _dot_f32(jnp.where(same, 1.0, 0.0), g)
    cum_t = cum.T
    e_cum = jnp.exp(cum)
    e_rest = jnp.exp(tot - cum)
    e_tot = jnp.exp(tot)

    r64 = lax.broadcasted_iota(jnp.int32, (CHUNK, CHUNK), 0)
    c64 = lax.broadcasted_iota(jnp.int32, (CHUNK, CHUNK), 1)
    tril = c64 <= r64
    stril = c64 < r64
    eye = jnp.where(c64 == r64, 1.0, 0.0)

    for h in range(HEADS):
        sl = slice(h * DH, (h + 1) * DH)
        qh = y[:, h * DH:(h + 1) * DH]
        kh = y[:, WIDTH + h * DH:WIDTH + (h + 1) * DH]
        vh = y[:, 2 * WIDTH + h * DH:2 * WIDTH + (h + 1) * DH]
        qh = qh * lax.rsqrt(jnp.sum(qh * qh, axis=-1, keepdims=True) + NORM_EPS) * (DH ** -0.5)
        kh = kh * lax.rsqrt(jnp.sum(kh * kh, axis=-1, keepdims=True) + NORM_EPS)
        bcol = beta[:, SUBLANES + h:SUBLANES + h + 1]
        ecum = e_cum[:, h:h + 1]
        qd_ref[:, sl] = (qh * ecum).astype(BF16)
        kd = kh * e_rest[:, h:h + 1]
        kdt_ref[h] = kd.T.astype(BF16)
        kb = kh * bcol
        rhs = jnp.concatenate([vh * bcol, kb * ecum], axis=1)
        for c in range(tm // CHUNK):
            rs = slice(c * CHUNK, (c + 1) * CHUNK)
            diff = cum[rs, h:h + 1] - cum_t[h:h + 1, rs]
            dec = jnp.exp(jnp.where(tril, diff, -jnp.inf))
            kcb = kh[rs].astype(BF16)
            a_mat = jnp.where(stril, _dot_nt(kb[rs].astype(BF16), kcb) * dec, 0.0)
            pw = -a_mat
            inv = eye + pw
            for _ in range(5):
                pwb = pw.astype(BF16)
                pw = _dot(pwb, pwb)
                inv = inv + _dot(inv.astype(BF16), pw.astype(BF16))
            sol = _dot(inv.astype(BF16), rhs[rs].astype(BF16))
            u_ref[rs, sl] = sol[:, :DH]
            w_ref[rs, sl] = sol[:, DH:].astype(BF16)
            aqk = jnp.where(tril, _dot_nt(qh[rs].astype(BF16), kcb) * dec, 0.0)
            aqk_ref[rs, h * CHUNK:(h + 1) * CHUNK] = aqk.astype(BF16)
            egl_ref[c:c + 1, sl] = jnp.broadcast_to(e_tot[c * CHUNK:c * CHUNK + 1, h:h + 1], (1, DH))


def _delta_prep(qkv_pre, ab, conv_w, alog_row, dtb_row, b, t):
    tm = MIX_TILE
    nt = t // tm
    nch = tm // CHUNK
    ncols = 3 * WIDTH
    hb = tm // SUBLANES
    xr = qkv_pre.reshape(b, t, ncols)
    abr = ab.reshape(b, t, LANES)
    return pl.pallas_call(
        _delta_prep_kernel,
        grid=(b, nt),
        in_specs=[pl.BlockSpec((None, tm, ncols), lambda bi, ti: (bi, ti, 0)),
                  pl.BlockSpec((None, SUBLANES, ncols), lambda bi, ti: (bi, jnp.maximum(ti * hb - 1, 0), 0)),
                  pl.BlockSpec((None, tm, LANES), lambda bi, ti: (bi, ti, 0)),
                  pl.BlockSpec((CONV_W, ncols), lambda bi, ti: (0, 0)),
                  pl.BlockSpec((1, LANES), lambda bi, ti: (0, 0)),
                  pl.BlockSpec((1, LANES), lambda bi, ti: (0, 0))],
        out_specs=[pl.BlockSpec((None, tm, WIDTH), lambda bi, ti: (bi, ti, 0)),
                   pl.BlockSpec((None, tm, WIDTH), lambda bi, ti: (bi, ti, 0)),
                   pl.BlockSpec((None, tm, WIDTH), lambda bi, ti: (bi, ti, 0)),
                   pl.BlockSpec((None, None, HEADS, DH, tm), lambda bi, ti: (bi, ti, 0, 0, 0)),
                   pl.BlockSpec((None, tm, HEADS * CHUNK), lambda bi, ti: (bi, ti, 0)),
                   pl.BlockSpec((None, None, nch, WIDTH), lambda bi, ti: (bi, ti, 0, 0))],
        out_shape=[jax.ShapeDtypeStruct((b, t, WIDTH), F32),
                   jax.ShapeDtypeStruct((b, t, WIDTH), BF16),
                   jax.ShapeDtypeStruct((b, t, WIDTH), BF16),
                   jax.ShapeDtypeStruct((b, nt, HEADS, DH, tm), BF16),
                   jax.ShapeDtypeStruct((b, t, HEADS * CHUNK), BF16),
                   jax.ShapeDtypeStruct((b, nt, nch, WIDTH), F32)],
        scratch_shapes=[pltpu.VMEM((tm + SUBLANES, ncols), F32)],
        compiler_params=_params(2),
        name="delta_prep",
    )(xr, xr, abr, conv_w, alog_row, dtb_row)


def _delta_scan_kernel(u_ref, w_ref, qd_ref, kdt_ref, aqk_ref, egl_ref, o_ref, s_out_ref, s_ref):
    ti = pl.program_id(1)

    @pl.when(ti == 0)
    def _():
        s_ref[...] = jnp.zeros_like(s_ref)

    tm = u_ref.shape[0]
    for c in range(tm // CHUNK):
        rs = slice(c * CHUNK, (c + 1) * CHUNK)
        for h in range(HEADS):
            sl = slice(h * DH, (h + 1) * DH)
            s = s_ref[h]
            sb = s.astype(BF16)
            vnew = u_ref[rs, sl] - _dot(w_ref[rs, sl], sb)
            vb = vnew.astype(BF16)
            o_ref[rs, sl] = _dot(qd_ref[rs, sl], sb) + _dot(aqk_ref[rs, h * CHUNK:(h + 1) * CHUNK], vb)
            s_ref[h] = s * egl_ref[c:c + 1, sl] + _dot(kdt_ref[h, :, rs], vb)

    @pl.when(ti == pl.num_programs(1) - 1)
    def _():
        s_out_ref[...] = s_ref[...]


def _delta_scan(u, w, qd, kdt, aqk, egl):
    b, t, _ = u.shape
    tm = MIX_TILE
    nt = t // tm
    nch = tm // CHUNK
    row = lambda bi, ti: (bi, ti, 0)
    return pl.pallas_call(
        _delta_scan_kernel,
        grid=(b, nt),
        in_specs=[pl.BlockSpec((None, tm, WIDTH), row),
                  pl.BlockSpec((None, tm, WIDTH), row),
                  pl.BlockSpec((None, tm, WIDTH), row),
                  pl.BlockSpec((None, None, HEADS, DH, tm), lambda bi, ti: (bi, ti, 0, 0, 0)),
                  pl.BlockSpec((None, tm, HEADS * CHUNK), row),
                  pl.BlockSpec((None, None, nch, WIDTH), lambda bi, ti: (bi, ti, 0, 0))],
        out_specs=[pl.BlockSpec((None, tm, WIDTH), row),
                   pl.BlockSpec((None, HEADS, DH, DH), lambda bi, ti: (bi, 0, 0, 0))],
        out_shape=[jax.ShapeDtypeStruct((b, t, WIDTH), F32),
                   jax.ShapeDtypeStruct((b, HEADS, DH, DH), F32)],
        scratch_shapes=[pltpu.VMEM((HEADS, DH, DH), F32)],
        compiler_params=_params(2),
        name="delta_scan",
    )(u, w, qd, kdt, aqk, egl)


def _delta_sample_kernel(x_ref, buf_ref, ab_ref, cw_ref, alog_ref, dtb_ref, s0_ref,
                         o_ref, s_out_ref, ext_ref, cols_ref, o_scr):
    steps = x_ref.shape[0]
    ext_ref[0:CONV_W - 1, :] = buf_ref[...]
    ext_ref[CONV_W - 1:CONV_W - 1 + steps, :] = x_ref[...]
    y = ext_ref[0:steps, :] * cw_ref[0:1, :]
    for j in range(1, CONV_W):
        y = y + ext_ref[j:j + steps, :] * cw_ref[j:j + 1, :]
    y = _silu(y)
    ab = ab_ref[...]
    eg = jnp.exp(-jnp.exp(alog_ref[...]) * jax.nn.softplus(ab + dtb_ref[...]))
    beta = jax.nn.sigmoid(ab)

    cols_ref[...] = jnp.zeros_like(cols_ref)
    vs = []
    for h in range(HEADS):
        qh = y[:, h * DH:(h + 1) * DH]
        kh = y[:, WIDTH + h * DH:WIDTH + (h + 1) * DH]
        qh = qh * lax.rsqrt(jnp.sum(qh * qh, axis=-1, keepdims=True) + NORM_EPS) * (DH ** -0.5)
        kh = kh * lax.rsqrt(jnp.sum(kh * kh, axis=-1, keepdims=True) + NORM_EPS)
        cols_ref[h * steps:(h + 1) * steps, :] = kh
        cols_ref[(HEADS + h) * steps:(HEADS + h + 1) * steps, :] = qh
        vs.append(y[:, 2 * WIDTH + h * DH:2 * WIDTH + (h + 1) * DH])
    cols = cols_ref[...].T

    for h in range(HEADS):
        s = s0_ref[h]
        for t in range(steps):
            kcol = cols[:, h * steps + t:h * steps + t + 1]
            qcol = cols[:, (HEADS + h) * steps + t:(HEADS + h) * steps + t + 1]
            s = s * eg[t:t + 1, h:h + 1]
            ks = jnp.sum(s * kcol, axis=0, keepdims=True)
            delta = beta[t:t + 1, SUBLANES + h:SUBLANES + h + 1] * (vs[h][t:t + 1, :] - ks)
            s = s + kcol * delta
            o_scr[t:t + 1, h * DH:(h + 1) * DH] = jnp.sum(s * qcol, axis=0, keepdims=True)
        s_out_ref[h] = s
    o_ref[...] = o_scr[0:steps, :]


def _delta_sample(qkv_pre, ab, conv_buf, conv_w, alog_row, dtb_row, state, layer, nb, steps):
    ncols = 3 * WIDTH
    xr = qkv_pre.reshape(nb, steps, ncols)
    abr = ab.reshape(nb, steps, LANES)
    return pl.pallas_call(
        _delta_sample_kernel,
        grid=(nb,),
        in_specs=[pl.BlockSpec((None, steps, ncols), lambda i: (i, 0, 0)),
                  pl.BlockSpec((None, None, CONV_W - 1, ncols), lambda i: (layer, i, 0, 0)),
                  pl.BlockSpec((None, steps, LANES), lambda i: (i, 0, 0)),
                  pl.BlockSpec((CONV_W, ncols), lambda i: (0, 0)),
                  pl.BlockSpec((1, LANES), lambda i: (0, 0)),
                  pl.BlockSpec((1, LANES), lambda i: (0, 0)),
                  pl.BlockSpec((None, None, HEADS, DH, DH), lambda i: (layer, i, 0, 0, 0))],
        out_specs=[pl.BlockSpec((None, steps, WIDTH), lambda i: (i, 0, 0)),
                   pl.BlockSpec((None, HEADS, DH, DH), lambda i: (i, 0, 0, 0))],
        out_shape=[jax.ShapeDtypeStruct((nb, steps, WIDTH), F32),
                   jax.ShapeDtypeStruct((nb, HEADS, DH, DH), F32)],
        scratch_shapes=[pltpu.VMEM((2 * SUBLANES, ncols), F32),
                        pltpu.VMEM((DH, DH), F32),
                        pltpu.VMEM((SUBLANES, WIDTH), F32)],
        compiler_params=_params(1),
        name="delta_sample",
    )(xr, conv_buf, abr, conv_w, alog_row, dtb_row, state)


def _attn_prompt_kernel(q_ref, kc_ref, kp_ref, vc_ref, vp_ref, o_ref, lse_ref):
    lt = q_ref.shape[0]
    li = pl.program_id(2)
    row = lax.broadcasted_iota(jnp.int32, (KEYS_BACK, KEYS_BACK), 0)
    col = lax.broadcasted_iota(jnp.int32, (KEYS_BACK, KEYS_BACK), 1)
    mask_cur = col <= row
    mask_prev = col >= row
    neg = -jnp.inf
    for blk in range(lt // KEYS_BACK):
        rs = slice(blk * KEYS_BACK, (blk + 1) * KEYS_BACK)
        lse_tile = jnp.zeros((KEYS_BACK, LANES), F32)
        for h in range(HEADS):
            sl = slice(h * DH, (h + 1) * DH)
            qb = q_ref[rs, sl]
            if blk == 0:
                kprev, vprev = kp_ref[:, sl], vp_ref[:, sl]
                mprev = jnp.logical_and(mask_prev, li > 0)
            else:
                ps = slice((blk - 1) * KEYS_BACK, blk * KEYS_BACK)
                kprev, vprev = kc_ref[ps, sl], vc_ref[ps, sl]
                mprev = mask_prev
            s_c = jnp.where(mask_cur, _dot_nt(qb, kc_ref[rs, sl]), neg)
            s_p = jnp.where(mprev, _dot_nt(qb, kprev), neg)
            m = jnp.maximum(jnp.max(s_c, axis=-1, keepdims=True), jnp.max(s_p, axis=-1, keepdims=True))
            p_c = jnp.exp(s_c - m)
            p_p = jnp.exp(s_p - m)
            den = jnp.sum(p_c, axis=-1, keepdims=True) + jnp.sum(p_p, axis=-1, keepdims=True)
            o = (_dot(p_c.astype(BF16), vc_ref[rs, sl]) + _dot(p_p.astype(BF16), vprev)) / den
            o_ref[rs, sl] = o.astype(o_ref.dtype)
            lse_tile = jnp.where(col == h, m + jnp.log(den), lse_tile)
        lse_ref[rs, :] = lse_tile


def _attn_prompt(q, k, v, g):
    b, dil, ln, _ = q.shape
    lt = min(ROW_TILE, ln)
    nb = lt // KEYS_BACK
    cur = pl.BlockSpec((None, None, lt, WIDTH), lambda bi, r, li: (bi, r, li, 0))
    prev = pl.BlockSpec((None, None, KEYS_BACK, WIDTH),
                        lambda bi, r, li: (bi, r, jnp.maximum(li * nb - 1, 0), 0))
    return pl.pallas_call(
        _attn_prompt_kernel,
        grid=(b, dil, ln // lt),
        in_specs=[cur, cur, prev, cur, prev],
        out_specs=[pl.BlockSpec((None, lt, WIDTH), lambda bi, r, li: (bi, li, r)),
                   pl.BlockSpec((None, lt, LANES), lambda bi, r, li: (bi, li, r))],
        out_shape=[jax.ShapeDtypeStruct((b, ln, dil * WIDTH), BF16),
                   jax.ShapeDtypeStruct((b, ln, dil * LANES), F32)],
        compiler_params=_params(3),
        name=f"attn_prompt_g{g}",
    )(q, k, k, v, v)


def _lane_sums(x):
    shp = x.shape
    flat = x.reshape(-1, LANES).astype(BF16)
    return _dot(flat, jnp.ones((LANES, LANES), BF16)).reshape(shp)


def _attn_sample_kernel(q_ref, k_ref, v_ref, c0_ref, c1_ref, c2_ref, o_ref):
    steps = q_ref.shape[1]
    outs, lses = [], []

    q0, k0, v0 = q_ref[0], k_ref[0], v_ref[0]
    kc, vc = c0_ref[:, 0], c0_ref[:, 1]
    ridx = lax.broadcasted_iota(jnp.int32, kc.shape, 0)
    o_rows, l_rows = [], []
    for s in range(steps):
        sc = jnp.where(ridx >= s, _lane_sums(kc * q0[s:s + 1]), -jnp.inf)
        sn = _lane_sums(k0[0:s + 1] * q0[s:s + 1])
        m = jnp.maximum(jnp.max(sc, axis=0, keepdims=True), jnp.max(sn, axis=0, keepdims=True))
        p = jnp.exp(sc - m)
        pn = jnp.exp(sn - m)
        den = jnp.sum(p, axis=0, keepdims=True) + jnp.sum(pn, axis=0, keepdims=True)
        num = jnp.sum(p * vc, axis=0, keepdims=True) + jnp.sum(pn * v0[0:s + 1], axis=0, keepdims=True)
        o_rows.append(num / den)
        l_rows.append(m + jnp.log(den))
    outs.append(jnp.concatenate(o_rows, axis=0))
    lses.append(jnp.concatenate(l_rows, axis=0))

    for gi, cref in ((1, c1_ref), (2, c2_ref)):
        qg, kg, vg = q_ref[gi], k_ref[gi], v_ref[gi]
        if gi == 1:
            dil = GROUPS[1][1]
            nrow = cref.shape[0] // dil
            kc = cref[:, 0].reshape(nrow, dil, HEADS, DH)[:, :steps]
            vc = cref[:, 1].reshape(nrow, dil, HEADS, DH)[:, :steps]
        else:
            kc, vc = cref[:, :, 0], cref[:, :, 1]
        sc = _lane_sums(kc * qg[None])
        sn = _lane_sums(kg * qg)
        m = jnp.maximum(jnp.max(sc, axis=0), sn)
        p = jnp.exp(sc - m[None])
        pn = jnp.exp(sn - m)
        den = jnp.sum(p, axis=0) + pn
        outs.append((jnp.sum(p * vc, axis=0) + pn * vg) / den)
        lses.append(m + jnp.log(den))

    m = jnp.maximum(jnp.maximum(lses[0], lses[1]), lses[2])
    es = [jnp.exp(l - m) for l in lses]
    den = es[0] + es[1] + es[2]
    o_ref[...] = (es[0] * outs[0] + es[1] * outs[1] + es[2] * outs[2]) / den


def _attn_sample(q, k, v, cache0, cache1, cache2, layer):
    nb, _, steps, _, _ = q.shape
    new = pl.BlockSpec((None, 3, steps, HEADS, DH), lambda i: (i, 0, 0, 0, 0))
    w0, w1 = cache0.shape[2], cache1.shape[2]
    dil2 = GROUPS[2][1]
    c2 = cache2.reshape(cache2.shape[0], nb, cache2.shape[2] // dil2, dil2, 2, HEADS, DH)
    return pl.pallas_call(
        _attn_sample_kernel,
        grid=(nb,),
        in_specs=[new, new, new,
                  pl.BlockSpec((None, None, w0, 2, HEADS, DH), lambda i: (layer, i, 0, 0, 0, 0)),
                  pl.BlockSpec((None, None, w1, 2, HEADS, DH), lambda i: (layer, i, 0, 0, 0, 0)),
                  pl.BlockSpec((None, None, c2.shape[2], steps, 2, HEADS, DH),
                               lambda i: (layer, i, 0, 0, 0, 0, 0))],
        out_specs=pl.BlockSpec((None, steps, HEADS, DH), lambda i: (i, 0, 0, 0)),
        out_shape=jax.ShapeDtypeStruct((nb, steps, HEADS, DH), F32),
        compiler_params=_params(1),
        name="attn_sample",
    )(q, k, v, cache0, cache1, c2)


def _pro_delta(refs, y_ref):
    o_ref, z_ref, og_ref = refs
    og = og_ref[...]
    for h in range(HEADS):
        sl = slice(h * DH, (h + 1) * DH)
        oh = o_ref[:, sl]
        oh = oh * lax.rsqrt(jnp.mean(oh * oh, axis=-1, keepdims=True) + NORM_EPS) * og
        y_ref[:, sl] = (oh * _silu(z_ref[:, sl].astype(F32))).astype(BF16)


def _pro_attn(refs, y_ref):
    o0, o1, o2, l0, l1, l2 = refs
    a, b, c = l0[...], l1[...], l2[...]
    m = jnp.maximum(jnp.maximum(a, b), c)
    ea, eb, ec = jnp.exp(a - m), jnp.exp(b - m), jnp.exp(c - m)
    den = ea + eb + ec
    wa, wb, wc = ea / den, eb / den, ec / den
    for h in range(HEADS):
        sl = slice(h * DH, (h + 1) * DH)
        y = (wa[:, h:h + 1] * o0[:, sl].astype(F32) + wb[:, h:h + 1] * o1[:, sl].astype(F32)
             + wc[:, h:h + 1] * o2[:, sl].astype(F32))
        y_ref[:, sl] = y.astype(BF16)


def _pro_plain(refs, y_ref):
    y_ref[...] = refs[0][...].astype(BF16)


def _route(logits_t):
    rows = [logits_t[e:e + 1, :] for e in range(N_EXPERTS)]
    mx = rows[0]
    for r in rows[1:]:
        mx = jnp.maximum(mx, r)
    ex = [jnp.exp(r - mx) for r in rows]
    gs = []
    for g in range(N_EXPERT_GROUPS):
        a, b, c, d = ex[4 * g:4 * g + 4]
        gs.append(jnp.maximum(jnp.maximum(jnp.maximum(a + b, a + c), jnp.maximum(a + d, b + c)),
                              jnp.maximum(b + d, c + d)))
    best = jnp.maximum(jnp.maximum(gs[0], gs[1]), jnp.maximum(gs[2], gs[3]))
    taken = jnp.zeros_like(best) > 1.0
    gsel = jnp.zeros_like(best)
    p = [jnp.zeros_like(best) for _ in range(EXPERTS_PER_GROUP)]
    for g in range(N_EXPERT_GROUPS):
        here = jnp.logical_and(jnp.logical_not(taken), gs[g] == best)
        taken = jnp.logical_or(taken, here)
        gsel = jnp.where(here, float(g), gsel)
        for k in range(EXPERTS_PER_GROUP):
            p[k] = jnp.where(here, ex[4 * g + k], p[k])

    def first_max(vals):
        top = jnp.maximum(jnp.maximum(vals[0], vals[1]), jnp.maximum(vals[2], vals[3]))
        found = jnp.zeros_like(top) > 1.0
        idx = jnp.zeros_like(top)
        for k in range(EXPERTS_PER_GROUP):
            here = jnp.logical_and(jnp.logical_not(found), vals[k] == top)
            found = jnp.logical_or(found, here)
            idx = jnp.where(here, float(k), idx)
        return top, idx

    v1, i1 = first_max(p)
    rest = [jnp.where(i1 == float(k), -1.0, p[k]) for k in range(EXPERTS_PER_GROUP)]
    v2, i2 = first_max(rest)
    lo = jnp.minimum(i1, i2)
    hi = jnp.maximum(i1, i2)
    p_lo = jnp.where(i1 < i2, v1, v2)
    p_hi = jnp.where(i1 < i2, v2, v1)
    pair = lo * (7.0 - lo) * 0.5 + hi - lo - 1.0
    cls = gsel * float(len(PAIRS)) + pair
    tot = p_lo + p_hi
    return cls, p_lo / tot, p_hi / tot


def _post_kernel(*refs, n_pro, prologue):
    pro = refs[:n_pro]
    (x_ref, wo_ref, g1_ref, nf_ref, sc_ref, sh_ref, rw_ref, rb_ref,
     x1_ref, h2_ref, route_ref, wob_ref, y_ref) = refs[n_pro:]

    @pl.when(pl.program_id(0) == 0)
    def _():
        _cast_rows(wob_ref, wo_ref)

    prologue(pro, y_ref)
    x1 = x_ref[...] + g1_ref[...] * _dot(y_ref[...], wob_ref[...])
    x1_ref[...] = x1
    h2 = _norm_mod(x1, nf_ref[...], sc_ref[...], sh_ref[...])
    h2_ref[...] = h2
    logits = _dot_f32(h2, rw_ref[...]) + rb_ref[...]
    cls, w_lo, w_hi = _route(logits.T)
    tm = x1.shape[0]
    route_ref[...] = jnp.concatenate([cls, w_lo, w_hi, jnp.zeros((SUBLANES - 3, tm), F32)], axis=0)


def _post(pro_arrays, pro_specs, prologue, x2, w_out, mod_arr, mod_specs, norm_ffn, rw_pad, rb_pad, tm, name):
    n, d = x2.shape
    g1_spec, sc_spec, sh_spec = mod_specs
    kernel = functools.partial(_post_kernel, n_pro=len(pro_arrays), prologue=prologue)
    const = lambda i: (0, 0)
    return pl.pallas_call(
        kernel,
        grid=(n // tm,),
        in_specs=list(pro_specs) + [
            pl.BlockSpec((tm, d), lambda i: (i, 0)),
            pl.BlockSpec((WIDTH, d), const),
            g1_spec,
            pl.BlockSpec((1, d), const),
            sc_spec, sh_spec,
            pl.BlockSpec((d, LANES), const),
            pl.BlockSpec((1, LANES), const)],
        out_specs=[pl.BlockSpec((tm, d), lambda i: (i, 0)),
                   pl.BlockSpec((tm, d), lambda i: (i, 0)),
                   pl.BlockSpec((SUBLANES, tm), lambda i: (0, i))],
        out_shape=[jax.ShapeDtypeStruct((n, d), F32),
                   jax.ShapeDtypeStruct((n, d), F32),
                   jax.ShapeDtypeStruct((SUBLANES, n), F32)],
        scratch_shapes=[pltpu.VMEM((WIDTH, d), BF16), pltpu.VMEM((tm, WIDTH), BF16)],
        compiler_params=_params(1),
        name=name,
    )(*pro_arrays, x2, w_out, mod_arr, norm_ffn, mod_arr, mod_arr, rw_pad, rb_pad)


def _row_copy(src_ref, dst_ref, sem, src_row, dst_row):
    return pltpu.make_async_copy(src_ref.at[pl.ds(src_row, 1), :], dst_ref.at[pl.ds(dst_row, 1), :], sem)


def _gather_into(idx_ref, base, src_ref, dst_ref, sem, tm):
    def issue(r, carry):
        _row_copy(src_ref, dst_ref, sem, idx_ref[base + r], r).start()
        return carry

    lax.fori_loop(0, tm, issue, 0, unroll=8)

    def drain(r, carry):
        _row_copy(src_ref, dst_ref, sem, 0, r).wait()
        return carry

    lax.fori_loop(0, tm, drain, 0, unroll=8)


def _gather_kernel(idx_ref, nused_ref, src_ref, o_ref, sem):
    tm = o_ref.shape[0]
    i = pl.program_id(0)

    @pl.when(i < nused_ref[0])
    def _():
        _gather_into(idx_ref, i * tm, src_ref, o_ref, sem, tm)

    @pl.when(i >= nused_ref[0])
    def _():
        o_ref[...] = jnp.zeros_like(o_ref)


def _gather_rows(src, idx, n_used_tiles, n_out):
    tm = GATHER_TILE
    width = src.shape[1]
    return pl.pallas_call(
        _gather_kernel,
        grid_spec=pltpu.PrefetchScalarGridSpec(
            num_scalar_prefetch=2,
            grid=(n_out // tm,),
            in_specs=[pl.BlockSpec(memory_space=pl.ANY)],
            out_specs=pl.BlockSpec((tm, width), lambda i, idx, nu: (i, 0)),
            scratch_shapes=[pltpu.SemaphoreType.DMA(())]),
        out_shape=jax.ShapeDtypeStruct((n_out, width), src.dtype),
        compiler_params=_params(1),
        name="gather_rows",
    )(idx, n_used_tiles, src)


def _combine_kernel(idx_ref, y_ref, x_ref, g2_ref, gain_ref, o_ref, buf, sem, *, final):
    tm = o_ref.shape[0]
    _gather_into(idx_ref, pl.program_id(0) * tm, y_ref, buf, sem, tm)
    x2 = x_ref[...] + g2_ref[...] * buf[...]
    if final:
        ms = jnp.mean(x2 * x2, axis=-1, keepdims=True)
        x2 = x2 * lax.rsqrt(ms + NORM_EPS) * gain_ref[...]
    o_ref[...] = x2


def _combine(y_sorted, pos, x1, mod_arr, g2_spec, gain, final, tm, name):
    n, d = x1.shape
    kernel = functools.partial(_combine_kernel, final=final)
    return pl.pallas_call(
        kernel,
        grid_spec=pltpu.PrefetchScalarGridSpec(
            num_scalar_prefetch=1,
            grid=(n // tm,),
            in_specs=[pl.BlockSpec(memory_space=pl.ANY),
                      pl.BlockSpec((tm, d), lambda i, idx: (i, 0)),
                      g2_spec,
                      pl.BlockSpec((1, d), lambda i, idx: (0, 0))],
            out_specs=pl.BlockSpec((tm, d), lambda i, idx: (i, 0)),
            scratch_shapes=[pltpu.VMEM((tm, d), F32), pltpu.SemaphoreType.DMA(())]),
        out_shape=jax.ShapeDtypeStruct((n, d), F32),
        compiler_params=_params(1),
        name=name,
    )(pos, y_sorted, x1, mod_arr, gain)


def _ffn_kernel(ea_ref, eb_ref, swap_ref, ca_ref, cb_ref, nu_ref,
                x_ref, rw_ref, ga_ref, ua_ref, da_ref, gb_ref, ub_ref, db_ref,
                o_ref, wa_gu, wa_d, wb_gu, wb_d):
    j = pl.program_id(0)
    f = ga_ref.shape[1]

    @pl.when(ca_ref[j] == 1)
    def _():
        _cast_rows(wa_gu, ga_ref, 0)
        _cast_rows(wa_gu, ua_ref, f)
        _cast_rows(wa_d, da_ref)

    @pl.when(cb_ref[j] == 1)
    def _():
        _cast_rows(wb_gu, gb_ref, 0)
        _cast_rows(wb_gu, ub_ref, f)
        _cast_rows(wb_d, db_ref)

    @pl.when(j < nu_ref[0])
    def _():
        x = x_ref[...].astype(BF16)
        w_lo = rw_ref[:, 1:2]
        w_hi = rw_ref[:, 2:3]
        swapped = swap_ref[j] == 1
        w_a = jnp.where(swapped, w_hi, w_lo)
        w_b = jnp.where(swapped, w_lo, w_hi)
        ha = _dot(x, wa_gu[...])
        act_a = (_silu(ha[:, :f]) * ha[:, f:] * w_a).astype(BF16)
        hb = _dot(x, wb_gu[...])
        act_b = (_silu(hb[:, :f]) * hb[:, f:] * w_b).astype(BF16)
        o_ref[...] = _dot(act_a, wa_d[...]) + _dot(act_b, wb_d[...])

    @pl.when(j >= nu_ref[0])
    def _():
        o_ref[...] = jnp.zeros_like(o_ref)


def _ffn(x_sorted, rw_sorted, w_gate, w_up, w_down, layer, tables):
    p, d = x_sorted.shape
    f = w_gate.shape[3]
    tm = FFN_TILE
    ea, eb, swap, ca, cb, nu = tables
    wa_in = pl.BlockSpec((None, None, d, f), lambda j, ea, eb, sw, ca, cb, nu: (layer, ea[j], 0, 0))
    wa_dn = pl.BlockSpec((None, None, f, d), lambda j, ea, eb, sw, ca, cb, nu: (layer, ea[j], 0, 0))
    wb_in = pl.BlockSpec((None, None, d, f), lambda j, ea, eb, sw, ca, cb, nu: (layer, eb[j], 0, 0))
    wb_dn = pl.BlockSpec((None, None, f, d), lambda j, ea, eb, sw, ca, cb, nu: (layer, eb[j], 0, 0))
    rows = lambda j, ea, eb, sw, ca, cb, nu: (j, 0)
    return pl.pallas_call(
        _ffn_kernel,
        grid_spec=pltpu.PrefetchScalarGridSpec(
            num_scalar_prefetch=6,
            grid=(p // tm,),
            in_specs=[pl.BlockSpec((tm, d), rows), pl.BlockSpec((tm, LANES), rows),
                      wa_in, wa_in, wa_dn, wb_in, wb_in, wb_dn],
            out_specs=pl.BlockSpec((tm, d), rows),
            scratch_shapes=[pltpu.VMEM((d, 2 * f), BF16), pltpu.VMEM((f, d), BF16),
                            pltpu.VMEM((d, 2 * f), BF16), pltpu.VMEM((f, d), BF16)]),
        out_shape=jax.ShapeDtypeStruct((p, d), F32),
        compiler_params=_params(1),
        name="moe_ffn",
    )(ea, eb, swap, ca, cb, nu, x_sorted, rw_sorted, w_gate, w_up, w_down, w_gate, w_up, w_down)


def _routing_tables(cls_f, n_rows_pad):
    n = cls_f.shape[0]
    cls = cls_f.astype(jnp.int32)
    visit_rank = [0] * len(PAIRS)
    for rank, (pair, _, _, _) in enumerate(VISIT):
        visit_rank[pair] = rank
    rank_tab = jnp.asarray(visit_rank, jnp.int32)
    npair = len(PAIRS)
    key = (cls // npair) * npair + rank_tab[cls % npair]
    onehot = (key[:, None] == jnp.arange(N_CLASSES, dtype=jnp.int32)[None, :]).astype(jnp.int32)
    csum = jnp.cumsum(onehot, axis=0)
    rank_in = jnp.sum(csum * onehot, axis=1) - 1
    counts = csum[-1]
    padded = ((counts + FFN_TILE - 1) // FFN_TILE) * FFN_TILE
    ends = jnp.cumsum(padded)
    offs = ends - padded
    pos = (offs[key] + rank_in).astype(jnp.int32)
    src = jnp.zeros((n_rows_pad,), jnp.int32).at[pos].set(jnp.arange(n, dtype=jnp.int32))
    n_tiles = n_rows_pad // FFN_TILE
    tile_key = jnp.minimum(jnp.searchsorted(ends // FFN_TILE, jnp.arange(n_tiles, dtype=jnp.int32), side="right"),
                           N_CLASSES - 1).astype(jnp.int32)
    grp = tile_key // npair
    vis = tile_key % npair
    vtab = jnp.asarray(VISIT, jnp.int32)
    ea = grp * EXPERTS_PER_GROUP + vtab[vis, 1]
    eb = grp * EXPERTS_PER_GROUP + vtab[vis, 2]
    swap = vtab[vis, 3]
    one = jnp.ones((1,), jnp.int32)
    ca = jnp.concatenate([one, (ea[1:] != ea[:-1]).astype(jnp.int32)])
    cb = jnp.concatenate([one, (eb[1:] != eb[:-1]).astype(jnp.int32)])
    nu = (ends[-1] // FFN_TILE).astype(jnp.int32).reshape(1)
    return pos, src, (ea.astype(jnp.int32), eb.astype(jnp.int32), swap, ca, cb, nu)


def kernel(x_prompt, x_sample, state_conv, state_delta, cache_kv0, cache_kv1, cache_kv2, c_prompt, c_sample,
           w_ada, b_ada, norm_mix, norm_ffn, norm_final, a_w_in, a_conv, a_log, a_dt_bias, a_out_norm, a_w_out,
           b_w_in, b_w_out, router_w, router_b, exp_w_gate, exp_w_up, exp_w_down):
    b, t, d = x_prompt.shape
    nb, steps, _ = x_sample.shape
    depth = w_ada.shape[0]
    n_p, n_s = b * t, nb * steps
    n_all = n_p + n_s
    assert d == WIDTH and t % ROW_TILE == 0 and n_s % 16 == 0
    assert steps <= CONV_W and cache_kv0.shape[2] == GROUPS[0][0] and cache_kv1.shape[2] == GROUPS[1][0]
    assert cache_kv2.shape[2] == GROUPS[2][0] and t % (GROUPS[2][1] * KEYS_BACK) == 0
    tiles_per_seq = t // ROW_TILE
    tm_s = n_s

    rows_c = b + nb
    rows_pad = -(-rows_c // SUBLANES) * SUBLANES
    c_all = jnp.concatenate([c_prompt, c_sample, jnp.zeros((rows_pad - rows_c, d), F32)], axis=0)
    mod = _adaln(c_all, w_ada, b_ada)
    mod_p = mod[:, :b].reshape(depth, b, 1, 6 * d)
    mod_s = jnp.repeat(mod[:, b:b + nb], steps, axis=1)

    rw_pad = jnp.pad(router_w, ((0, 0), (0, LANES - N_EXPERTS)))
    rb_pad = jnp.pad(router_b, (0, LANES - N_EXPERTS), constant_values=-1e30).reshape(1, LANES)
    tabs_p = _rope_tables(jnp.arange(t, dtype=jnp.int32))
    past = cache_kv2.shape[2]
    tabs_s = _rope_tables(past + (jnp.arange(n_s, dtype=jnp.int32) % steps))

    n_rows_pad = n_all + N_CLASSES * FFN_TILE
    n_rows_pad = -(-n_rows_pad // GATHER_TILE) * GATHER_TILE

    xp = x_prompt.reshape(n_p, d)
    xs = x_sample.reshape(n_s, d)
    convs_p, deltas_p, convs_s, deltas_s = [], [], [], []
    kvs_p = [[] for _ in GROUPS]
    kvs_s = [[] for _ in GROUPS]
    y_p = y_s = None

    for i in range(depth):
        j = i // N_MIXERS
        gain_mix = norm_mix[i].reshape(1, d)
        gain_ffn = norm_ffn[i].reshape(1, d)
        mp, ms = mod_p[i], mod_s[i]
        sc_p, sh_p = _mod_specs_prompt(tiles_per_seq, d, 1, 0)
        sc_s, sh_s = _mod_specs_rows(tm_s, d, 1, 0)
        post_mod_p = (pl.BlockSpec((None, 1, d), lambda r: (r // tiles_per_seq, 0, 2)),
                      pl.BlockSpec((None, 1, d), lambda r: (r // tiles_per_seq, 0, 4)),
                      pl.BlockSpec((None, 1, d), lambda r: (r // tiles_per_seq, 0, 3)))
        post_mod_s = (pl.BlockSpec((tm_s, d), lambda r: (r, 2)),
                      pl.BlockSpec((tm_s, d), lambda r: (r, 4)),
                      pl.BlockSpec((tm_s, d), lambda r: (r, 3)))

        if i % N_MIXERS == 0:
            ncols = 3 * WIDTH
            w_qkv = a_w_in[j]
            zab_cols = WIDTH + LANES
            w_zab = jnp.pad(a_w_in[j][:, ncols:], ((0, 0), (0, zab_cols - (a_w_in.shape[2] - ncols))))
            alog_row = jnp.pad(a_log[j], (0, LANES - HEADS)).reshape(1, LANES)
            dtb_row = jnp.pad(a_dt_bias[j], (0, LANES - HEADS)).reshape(1, LANES)
            og_row = a_out_norm[j].reshape(1, DH)
            wq_spec = pl.BlockSpec((d, ncols), lambda r: (0, 0))
            wz_spec = pl.BlockSpec((d, zab_cols), lambda r: (0, 0))

            qkv_p, tail_p = _inproj_rows(
                xp, gain_mix, mp, mp, sc_p, sh_p, w_qkv, wq_spec, ncols, (), (),
                [jax.ShapeDtypeStruct((n_p, ncols), BF16), jax.ShapeDtypeStruct((b, SUBLANES, ncols), F32)],
                [pl.BlockSpec((ROW_TILE, ncols), lambda r: (r, 0)),
                 pl.BlockSpec((None, SUBLANES, ncols), lambda r: (r // tiles_per_seq, 0, 0))],
                functools.partial(_epi_delta_qkv, with_tail=True), ROW_TILE, "inproj_delta_qkv")
            z_p, ab_p = _inproj_rows(
                xp, gain_mix, mp, mp, sc_p, sh_p, w_zab, wz_spec, zab_cols, (), (),
                [jax.ShapeDtypeStruct((n_p, WIDTH), BF16), jax.ShapeDtypeStruct((n_p, LANES), F32)],
                [pl.BlockSpec((ROW_TILE, WIDTH), lambda r: (r, 0)),
                 pl.BlockSpec((ROW_TILE, LANES), lambda r: (r, 0))],
                _epi_delta_zab, ROW_TILE, "inproj_delta_zab")
            convs_p.append(tail_p[:, SUBLANES - (CONV_W - 1):, :])
            u, w, qd, kdt, aqk, egl = _delta_prep(qkv_p, ab_p, a_conv[j], alog_row, dtb_row, b, t)
            o_p, s_p = _delta_scan(u, w, qd, kdt, aqk, egl)
            deltas_p.append(s_p)
            pro_p = ([o_p.reshape(n_p, WIDTH), z_p, og_row],
                     [pl.BlockSpec((ROW_TILE, WIDTH), lambda r: (r, 0)),
                      pl.BlockSpec((ROW_TILE, WIDTH), lambda r: (r, 0)),
                      pl.BlockSpec((1, DH), lambda r: (0, 0))], _pro_delta)

            qkv_s, = _inproj_rows(
                xs, gain_mix, ms, ms, sc_s, sh_s, w_qkv, wq_spec, ncols, (), (),
                [jax.ShapeDtypeStruct((n_s, ncols), F32)],
                [pl.BlockSpec((tm_s, ncols), lambda r: (r, 0))],
                functools.partial(_epi_delta_qkv, with_tail=False), tm_s, "inproj_delta_qkv_s")
            z_s, ab_s = _inproj_rows(
                xs, gain_mix, ms, ms, sc_s, sh_s, w_zab, wz_spec, zab_cols, (), (),
                [jax.ShapeDtypeStruct((n_s, WIDTH), BF16), jax.ShapeDtypeStruct((n_s, LANES), F32)],
                [pl.BlockSpec((tm_s, WIDTH), lambda r: (r, 0)),
                 pl.BlockSpec((tm_s, LANES), lambda r: (r, 0))],
                _epi_delta_zab, tm_s, "inproj_delta_zab_s")
            convs_s.append(qkv_s.reshape(nb, steps, ncols)[:, steps - (CONV_W - 1):, :])
            o_s, s_s = _delta_sample(qkv_s, ab_s, state_conv, a_conv[j], alog_row, dtb_row, state_delta,
                                     j, nb, steps)
            deltas_s.append(s_s)
            pro_s = ([o_s.reshape(n_s, WIDTH), z_s, og_row],
                     [pl.BlockSpec((tm_s, WIDTH), lambda r: (r, 0)),
                      pl.BlockSpec((tm_s, WIDTH), lambda r: (r, 0)),
                      pl.BlockSpec((1, DH), lambda r: (0, 0))], _pro_delta)
            w_out = a_w_out[j]
        else:
            x3 = xp.reshape(b, t, d)
            os_, ls_ = [], []
            for g, (win, dil) in enumerate(GROUPS):
                q, k, v = _inproj_attn_prompt(x3, gain_mix, mp, b_w_in[j], g, dil, tabs_p)
                o_g, lse_g = _attn_prompt(q, k, v, g)
                os_.append(o_g.reshape(n_p, WIDTH))
                ls_.append(lse_g.reshape(n_p, LANES))
                keep = min(win, t)
                lk = keep // dil
                ln = t // dil

                def tail(a):
                    a = a[:, :, ln - lk:, :]
                    return jnp.transpose(a, (0, 2, 1, 3)).reshape(b, keep, HEADS, DH)

                kvs_p[g].append(jnp.stack([tail(k), tail(v)], axis=2).astype(F32))
            row_spec = pl.BlockSpec((ROW_TILE, WIDTH), lambda r: (r, 0))
            lse_spec = pl.BlockSpec((ROW_TILE, LANES), lambda r: (r, 0))
            pro_p = (os_ + ls_, [row_spec] * 3 + [lse_spec] * 3, _pro_attn)

            qs, ks, vs = [], [], []
            for g in range(len(GROUPS)):
                outs = _inproj_rows(
                    xs, gain_mix, ms, ms, sc_s, sh_s, b_w_in[j],
                    pl.BlockSpec((d, 3 * WIDTH), lambda r, g=g: (0, g)), 3 * WIDTH,
                    tabs_s, [pl.BlockSpec((tm_s, DH), lambda r: (r, 0))] * 3,
                    [jax.ShapeDtypeStruct((n_s, WIDTH), F32)] * 3,
                    [pl.BlockSpec((tm_s, WIDTH), lambda r: (r, 0))] * 3,
                    functools.partial(_epi_attn, out_dtype=F32), tm_s, f"inproj_attn_s_g{g}")
                qs.append(outs[0].reshape(nb, steps, HEADS, DH))
                ks.append(outs[1].reshape(nb, steps, HEADS, DH))
                vs.append(outs[2].reshape(nb, steps, HEADS, DH))
                kvs_s[g].append(jnp.stack([ks[-1], vs[-1]], axis=2))
            y_attn_s = _attn_sample(jnp.stack(qs, axis=1), jnp.stack(ks, axis=1), jnp.stack(vs, axis=1),
                                    cache_kv0, cache_kv1, cache_kv2, j)
            pro_s = ([y_attn_s.reshape(n_s, WIDTH)],
                     [pl.BlockSpec((tm_s, WIDTH), lambda r: (r, 0))], _pro_plain)
            w_out = b_w_out[j]

        x1_p, h2_p, route_p = _post(pro_p[0], pro_p[1], pro_p[2], xp, w_out, mp, post_mod_p,
                                    gain_ffn, rw_pad, rb_pad, ROW_TILE, "post_prompt")
        x1_s, h2_s, route_s = _post(pro_s[0], pro_s[1], pro_s[2], xs, w_out, ms, post_mod_s,
                                    gain_ffn, rw_pad, rb_pad, tm_s, "post_sample")

        h2 = jnp.concatenate([h2_p, h2_s], axis=0)
        route = jnp.concatenate([route_p, route_s], axis=1)
        pos, src, tables = _routing_tables(route[0], n_rows_pad)
        nu_g = (tables[5] * FFN_TILE + GATHER_TILE - 1) // GATHER_TILE
        route_rows = jnp.pad(route.T, ((0, 0), (0, LANES - SUBLANES)))
        x_sorted = _gather_rows(h2, src, nu_g, n_rows_pad)
        rw_sorted = _gather_rows(route_rows, src, nu_g, n_rows_pad)
        y_sorted = _ffn(x_sorted, rw_sorted, exp_w_gate, exp_w_up, exp_w_down, i, tables)

        final = i == depth - 1
        gain_fin = norm_final.reshape(1, d)
        g2_p = pl.BlockSpec((None, 1, d), lambda r, idx: (r // (t // GATHER_TILE), 0, 5))
        tm_cs = GATHER_TILE if n_s % GATHER_TILE == 0 else n_s
        g2_s = pl.BlockSpec((tm_cs, d), lambda r, idx: (r, 5))
        xp = _combine(y_sorted, pos[:n_p], x1_p, mp, g2_p, gain_fin, final, GATHER_TILE, "combine_prompt")
        xs = _combine(y_sorted, pos[n_p:], x1_s, ms, g2_s, gain_fin, final, tm_cs, "combine_sample")

    y_p = xp.reshape(b, t, d)
    y_s = xs.reshape(nb, steps, d)
    return (y_p, y_s, jnp.stack(convs_p), jnp.stack(deltas_p),
            jnp.stack(kvs_p[0]), jnp.stack(kvs_p[1]), jnp.stack(kvs_p[2]),
            jnp.stack(convs_s), jnp.stack(deltas_s),
            jnp.stack(kvs_s[0]), jnp.stack(kvs_s[1]), jnp.stack(kvs_s[2]))
```

```python
import functools

import jax
import jax.numpy as jnp
from jax import lax
from jax.experimental import pallas as pl
from jax.experimental.pallas import tpu as pltpu

F32 = jnp.float32
BF16 = jnp.bfloat16
I32 = jnp.int32

N_MIXERS = 2
HEADS = 8
DH = 128
WIDTH = HEADS * DH
CONV_W = 4
CHUNK = 64
GROUPS = ((128, 1), (512, 4), (2048, 16))
KEYS_BACK = 128
ROT_DIM = DH // 4
ROPE_THETA = 500000.0
N_EXPERTS = 16
N_EXPERT_GROUPS = 4
EXPERTS_PER_GROUP = 4
NORM_EPS = 1e-6

LANES = 128
SUBLANES = 8
VMEM_LIMIT_BYTES = 58 * 1024 * 1024

ROW_TILE = 512
MIX_TILE = 256
FFN_TILE = 256

PAIRS = ((0, 1), (0, 2), (0, 3), (1, 2), (1, 3), (2, 3))
VISIT = ((0, 0, 1, 0), (3, 2, 1, 1), (5, 2, 3, 0), (2, 0, 3, 0), (1, 0, 2, 0), (4, 1, 3, 0))
N_CLASSES = N_EXPERT_GROUPS * len(PAIRS)
CLASS_SLOTS = 32
SORT_ROWS = ROW_TILE + 2 * LANES
SORT_COLS = WIDTH + LANES
assert SORT_ROWS >= ROW_TILE + N_CLASSES * (SUBLANES - 1)


def _params(n_grid):
    return pltpu.CompilerParams(dimension_semantics=("arbitrary",) * n_grid,
                                vmem_limit_bytes=VMEM_LIMIT_BYTES)


def _first_step(n_grid):
    first = pl.program_id(0) == 0
    for a in range(1, n_grid):
        first = jnp.logical_and(first, pl.program_id(a) == 0)
    return first


def _silu(x):
    return x * jax.nn.sigmoid(x)


def _norm_mod(x, gain, sc, sh):
    ms = jnp.mean(x * x, axis=-1, keepdims=True)
    return (x * lax.rsqrt(ms + NORM_EPS) * gain) * (1.0 + sc) + sh


def _dot(a, b):
    return jnp.dot(a, b, preferred_element_type=F32)


def _dot_nt(a, b):
    return lax.dot_general(a, b, (((1,), (1,)), ((), ())), preferred_element_type=F32)


def _split2(a):
    hi = a.astype(BF16)
    return hi, (a - hi.astype(F32)).astype(BF16)


def _split3(a):
    hi = a.astype(BF16)
    r = a - hi.astype(F32)
    mid = r.astype(BF16)
    return hi, mid, (r - mid.astype(F32)).astype(BF16)


def _dot_2x2(a, b):
    a_hi, a_lo = _split2(a)
    b_hi, b_lo = _split2(b)
    return _dot(a_hi, b_hi) + (_dot(a_lo, b_hi) + _dot(a_hi, b_lo))


def _dot_exact_lhs(a_b, b):
    b_hi, b_mid, b_lo = _split3(b)
    return _dot(a_b, b_hi) + (_dot(a_b, b_mid) + _dot(a_b, b_lo))


def _cast_rows(dst_ref, src_ref, col0=0):
    rows, ncols = src_ref.shape
    step = 128 if rows % 128 == 0 else rows

    def body(c, carry):
        r0 = pl.multiple_of(c * step, step)
        dst_ref[pl.ds(r0, step), col0:col0 + ncols] = src_ref[pl.ds(r0, step), :].astype(BF16)
        return carry

    lax.fori_loop(0, rows // step, body, 0)


def _adaln_kernel(c_ref, w_ref, b_ref, o_ref):
    o_ref[...] = _dot_2x2(_silu(c_ref[...]), w_ref[...]) + b_ref[...]


def _adaln(c_all, w_ada, b_ada):
    rows, d = c_all.shape
    depth, _, six_d = w_ada.shape
    tn = 1024
    return pl.pallas_call(
        _adaln_kernel,
        grid=(depth, six_d // tn),
        in_specs=[pl.BlockSpec((rows, d), lambda l, j: (0, 0)),
                  pl.BlockSpec((None, d, tn), lambda l, j: (l, 0, j)),
                  pl.BlockSpec((None, 1, tn), lambda l, j: (l, 0, j))],
        out_specs=pl.BlockSpec((None, rows, tn), lambda l, j: (l, 0, j)),
        out_shape=jax.ShapeDtypeStruct((depth, rows, six_d), F32),
        compiler_params=_params(2),
        name="adaln",
    )(c_all, w_ada, b_ada.reshape(depth, 1, six_d))


def _inproj_kernel(x_ref, g_ref, sc_ref, sh_ref, w_ref, *rest, n_grid, n_extra, n_out, epilogue):
    extra = rest[:n_extra]
    outs = rest[n_extra:n_extra + n_out]
    wb_ref = rest[n_extra + n_out]

    @pl.when(_first_step(n_grid))
    def _():
        _cast_rows(wb_ref, w_ref)

    h = _norm_mod(x_ref[...], g_ref[...], sc_ref[...], sh_ref[...])
    acc = _dot(h.astype(BF16), wb_ref[...])
    epilogue(acc, h, w_ref, extra, outs)


def _rope_heads(x, cos, sin_hi, sin_lo):
    parts = []
    for h in range(HEADS):
        xh = x[:, h * DH:(h + 1) * DH]
        parts.append(xh * cos + pltpu.roll(xh, DH - ROT_DIM // 2, 1) * sin_hi
                     + pltpu.roll(xh, ROT_DIM // 2, 1) * sin_lo)
    return jnp.concatenate(parts, axis=1)


def _epi_attn(acc, h, w_ref, extra, outs, *, out_dtype):
    cos_ref, shi_ref, slo_ref = extra
    q_ref, k_ref, v_ref = outs
    cos, shi, slo = cos_ref[...], shi_ref[...], slo_ref[...]
    q = _rope_heads(acc[:, :WIDTH], cos, shi, slo) * (DH ** -0.5)
    k = _rope_heads(acc[:, WIDTH:2 * WIDTH], cos, shi, slo)
    q_ref[...] = q.astype(out_dtype)
    k_ref[...] = k.astype(out_dtype)
    v_ref[...] = acc[:, 2 * WIDTH:].astype(out_dtype)


def _epi_delta_qkv(acc, h, w_ref, extra, outs, *, with_tail):
    outs[0][...] = acc.astype(outs[0].dtype)
    if with_tail:
        rows = acc.shape[0]
        outs[1][...] = acc[rows - SUBLANES:, :]


def _epi_delta_zab(acc, h, w_ref, extra, outs):
    outs[0][...] = acc[:, :WIDTH].astype(outs[0].dtype)
    outs[1][...] = acc[:, WIDTH:]


def _rope_tables(pos):
    half = ROT_DIM // 2
    inv = ROPE_THETA ** (-jnp.arange(half, dtype=F32) / half)
    ang = pos.astype(F32)[:, None] * inv[None, :]
    cos, sin = jnp.cos(ang), jnp.sin(ang)
    n = pos.shape[0]
    pad = jnp.zeros((n, DH - ROT_DIM), F32)
    z = jnp.zeros((n, half), F32)
    cos_t = jnp.concatenate([cos, cos, jnp.ones((n, DH - ROT_DIM), F32)], axis=1)
    sin_hi = jnp.concatenate([-sin, z, pad], axis=1)
    sin_lo = jnp.concatenate([z, sin, pad], axis=1)
    return cos_t, sin_hi, sin_lo


def _inproj_attn_prompt(x, gain, mod3, w_in, g, dil, tables):
    b, t, d = x.shape
    ln = t // dil
    lt = min(ROW_TILE, ln)
    x3 = x.reshape(b, ln, dil * d)
    tabs = [tb.reshape(ln, dil * DH) for tb in tables]
    ncols = 3 * WIDTH
    kernel = functools.partial(_inproj_kernel, n_grid=3, n_extra=3, n_out=3,
                               epilogue=functools.partial(_epi_attn, out_dtype=BF16))
    out_sd = jax.ShapeDtypeStruct((b, dil, ln, WIDTH), BF16)
    out_spec = pl.BlockSpec((None, None, lt, WIDTH), lambda bi, r, li: (bi, r, li, 0))
    tab_spec = pl.BlockSpec((lt, DH), lambda bi, r, li: (li, r))
    return pl.pallas_call(
        kernel,
        grid=(b, dil, ln // lt),
        in_specs=[pl.BlockSpec((None, lt, d), lambda bi, r, li: (bi, li, r)),
                  pl.BlockSpec((1, d), lambda bi, r, li: (0, 0)),
                  pl.BlockSpec((None, 1, d), lambda bi, r, li: (bi, 0, 1)),
                  pl.BlockSpec((None, 1, d), lambda bi, r, li: (bi, 0, 0)),
                  pl.BlockSpec((d, ncols), lambda bi, r, li: (0, g)),
                  tab_spec, tab_spec, tab_spec],
        out_specs=[out_spec, out_spec, out_spec],
        out_shape=[out_sd, out_sd, out_sd],
        scratch_shapes=[pltpu.VMEM((d, ncols), BF16)],
        compiler_params=_params(3),
        name=f"inproj_attn_g{g}",
    )(x3, gain, mod3, mod3, w_in, *tabs)


def _inproj_rows(x2, row0, n_rows, gain, sc_arr, sh_arr, sc_spec, sh_spec, w, wspec, ncols, extra, extra_specs,
                 out_shapes, out_specs, epilogue, tm, name):
    d = x2.shape[1]
    kernel = functools.partial(_inproj_kernel, n_grid=1, n_extra=len(extra), n_out=len(out_shapes),
                               epilogue=epilogue)
    return pl.pallas_call(
        kernel,
        grid=(n_rows // tm,),
        in_specs=[pl.BlockSpec((tm, d), lambda i: (row0 + i, 0)),
                  pl.BlockSpec((1, d), lambda i: (0, 0)),
                  sc_spec, sh_spec, wspec] + list(extra_specs),
        out_specs=out_specs,
        out_shape=out_shapes,
        scratch_shapes=[pltpu.VMEM((d, ncols), BF16)],
        compiler_params=_params(1),
        name=name,
    )(x2, gain, sc_arr, sh_arr, w, *extra)


def _delta_prep_kernel(x_ref, halo_ref, ab_ref, cw_ref, alog_ref, dtb_ref,
                       u_ref, w_ref, qd_ref, kdt_ref, aqk_ref, egl_ref, ext_ref):
    tm = x_ref.shape[0]
    ti = pl.program_id(1)
    halo = halo_ref[...].astype(F32)
    ext_ref[0:SUBLANES, :] = jnp.where(ti > 0, halo, 0.0)
    ext_ref[SUBLANES:, :] = x_ref[...].astype(F32)
    y = ext_ref[pl.ds(SUBLANES - (CONV_W - 1), tm), :] * cw_ref[0:1, :]
    for j in range(1, CONV_W):
        y = y + ext_ref[pl.ds(SUBLANES - (CONV_W - 1) + j, tm), :] * cw_ref[j:j + 1, :]
    y = _silu(y)

    ab = ab_ref[...]
    g = -jnp.exp(alog_ref[...]) * jax.nn.softplus(ab + dtb_ref[...])
    beta = jax.nn.sigmoid(ab)
    row = lax.broadcasted_iota(I32, (tm, tm), 0)
    col = lax.broadcasted_iota(I32, (tm, tm), 1)
    same = (row // CHUNK) == (col // CHUNK)
    tril = jnp.logical_and(same, col <= row)
    stril = jnp.logical_and(same, col < row)
    cum = _dot_exact_lhs(jnp.where(tril, 1.0, 0.0).astype(BF16), g)
    tot = _dot_exact_lhs(jnp.where(same, 1.0, 0.0).astype(BF16), g)
    cum_t = cum.T
    e_cum = jnp.exp(cum)
    e_rest = jnp.exp(tot - cum)
    e_tot = jnp.exp(tot)

    pws, sols = [], []
    for h in range(HEADS):
        sl = slice(h * DH, (h + 1) * DH)
        qh = y[:, h * DH:(h + 1) * DH]
        kh = y[:, WIDTH + h * DH:WIDTH + (h + 1) * DH]
        vh = y[:, 2 * WIDTH + h * DH:2 * WIDTH + (h + 1) * DH]
        qh = qh * lax.rsqrt(jnp.sum(qh * qh, axis=-1, keepdims=True) + NORM_EPS) * (DH ** -0.5)
        kh = kh * lax.rsqrt(jnp.sum(kh * kh, axis=-1, keepdims=True) + NORM_EPS)
        bcol = beta[:, SUBLANES + h:SUBLANES + h + 1]
        ecum = e_cum[:, h:h + 1]
        qd_ref[:, sl] = (qh * ecum).astype(BF16)
        kdt_ref[h] = (kh * e_rest[:, h:h + 1]).T.astype(BF16)
        kb = kh * bcol
        khb = kh.astype(BF16)
        dec = jnp.exp(jnp.where(tril, cum[:, h:h + 1] - cum_t[h:h + 1, :], -jnp.inf))
        pws.append(-jnp.where(stril, _dot_nt(kb.astype(BF16), khb) * dec, 0.0))
        aqk = jnp.where(tril, _dot_nt(qh.astype(BF16), khb) * dec, 0.0)
        for c in range(tm // CHUNK):
            rs = slice(c * CHUNK, (c + 1) * CHUNK)
            aqk_ref[rs, h * CHUNK:(h + 1) * CHUNK] = aqk[rs, rs].astype(BF16)
            egl_ref[c:c + 1, sl] = jnp.broadcast_to(e_tot[c * CHUNK:c * CHUNK + 1, h:h + 1], (1, DH))
        sols.append(jnp.concatenate([vh * bcol, kb * ecum], axis=1))

    eye = jnp.where(row == col, 1.0, 0.0)
    invs = [eye + p for p in pws]
    for _ in range(CHUNK.bit_length() - 2):
        pbs = [p.astype(BF16) for p in pws]
        pws = [_dot(pb, pb) for pb in pbs]
        invs = [t + _dot(t.astype(BF16), p.astype(BF16)) for t, p in zip(invs, pws)]
    for h in range(HEADS):
        sl = slice(h * DH, (h + 1) * DH)
        sol = _dot(invs[h].astype(BF16), sols[h].astype(BF16))
        u_ref[:, sl] = sol[:, :DH]
        w_ref[:, sl] = sol[:, DH:].astype(BF16)


def _delta_prep(qkv_pre, ab, conv_w, alog_row, dtb_row, b, t):
    tm = MIX_TILE
    nt = t // tm
    nch = tm // CHUNK
    ncols = 3 * WIDTH
    hb = tm // SUBLANES
    row = lambda bi, ti: (bi * nt + ti, 0)
    return pl.pallas_call(
        _delta_prep_kernel,
        grid=(b, nt),
        in_specs=[pl.BlockSpec((tm, ncols), row),
                  pl.BlockSpec((SUBLANES, ncols), lambda bi, ti: (jnp.maximum((bi * nt + ti) * hb - 1, 0), 0)),
                  pl.BlockSpec((tm, LANES), row),
                  pl.BlockSpec((CONV_W, ncols), lambda bi, ti: (0, 0)),
                  pl.BlockSpec((1, LANES), lambda bi, ti: (0, 0)),
                  pl.BlockSpec((1, LANES), lambda bi, ti: (0, 0))],
        out_specs=[pl.BlockSpec((tm, WIDTH), row),
                   pl.BlockSpec((tm, WIDTH), row),
                   pl.BlockSpec((tm, WIDTH), row),
                   pl.BlockSpec((None, HEADS, DH, tm), lambda bi, ti: (bi * nt + ti, 0, 0, 0)),
                   pl.BlockSpec((tm, HEADS * CHUNK), row),
                   pl.BlockSpec((None, nch, WIDTH), lambda bi, ti: (bi * nt + ti, 0, 0))],
        out_shape=[jax.ShapeDtypeStruct((b * t, WIDTH), F32),
                   jax.ShapeDtypeStruct((b * t, WIDTH), BF16),
                   jax.ShapeDtypeStruct((b * t, WIDTH), BF16),
                   jax.ShapeDtypeStruct((b * nt, HEADS, DH, tm), BF16),
                   jax.ShapeDtypeStruct((b * t, HEADS * CHUNK), BF16),
                   jax.ShapeDtypeStruct((b * nt, nch, WIDTH), F32)],
        scratch_shapes=[pltpu.VMEM((tm + SUBLANES, ncols), F32)],
        compiler_params=_params(2),
        name="delta_prep",
    )(qkv_pre, qkv_pre, ab, conv_w, alog_row, dtb_row)


def _delta_scan_kernel(u_ref, w_ref, qd_ref, kdt_ref, aqk_ref, egl_ref, o_ref, s_out_ref, s_ref):
    ti = pl.program_id(1)

    @pl.when(ti == 0)
    def _():
        s_ref[...] = jnp.zeros_like(s_ref)

    tm = u_ref.shape[0]
    heads = range(HEADS)
    sls = [slice(h * DH, (h + 1) * DH) for h in heads]
    for c in range(tm // CHUNK):
        rs = slice(c * CHUNK, (c + 1) * CHUNK)
        ss = [s_ref[h] for h in heads]
        sbs = [s.astype(BF16) for s in ss]
        ws = [_dot(w_ref[rs, sls[h]], sbs[h]) for h in heads]
        qs = [_dot(qd_ref[rs, sls[h]], sbs[h]) for h in heads]
        vbs = [(u_ref[rs, sls[h]] - ws[h]).astype(BF16) for h in heads]
        for h in heads:
            o_ref[rs, sls[h]] = qs[h] + _dot(aqk_ref[rs, h * CHUNK:(h + 1) * CHUNK], vbs[h])
        for h in heads:
            s_ref[h] = ss[h] * egl_ref[c:c + 1, sls[h]] + _dot(kdt_ref[h, :, rs], vbs[h])

    @pl.when(ti == pl.num_programs(1) - 1)
    def _():
        s_out_ref[...] = s_ref[...]


def _delta_scan(u, w, qd, kdt, aqk, egl, b, t):
    tm = MIX_TILE
    nt = t // tm
    nch = tm // CHUNK
    row = lambda bi, ti: (bi * nt + ti, 0)
    return pl.pallas_call(
        _delta_scan_kernel,
        grid=(b, nt),
        in_specs=[pl.BlockSpec((tm, WIDTH), row),
                  pl.BlockSpec((tm, WIDTH), row),
                  pl.BlockSpec((tm, WIDTH), row),
                  pl.BlockSpec((None, HEADS, DH, tm), lambda bi, ti: (bi * nt + ti, 0, 0, 0)),
                  pl.BlockSpec((tm, HEADS * CHUNK), row),
                  pl.BlockSpec((None, nch, WIDTH), lambda bi, ti: (bi * nt + ti, 0, 0))],
        out_specs=[pl.BlockSpec((tm, WIDTH), row),
                   pl.BlockSpec((None, HEADS, DH, DH), lambda bi, ti: (bi, 0, 0, 0))],
        out_shape=[jax.ShapeDtypeStruct((b * t, WIDTH), F32),
                   jax.ShapeDtypeStruct((b, HEADS, DH, DH), F32)],
        scratch_shapes=[pltpu.VMEM((HEADS, DH, DH), F32)],
        compiler_params=_params(2),
        name="delta_scan",
    )(u, w, qd, kdt, aqk, egl)


def _delta_sample_kernel(x_ref, buf_ref, ab_ref, cw_ref, alog_ref, dtb_ref, s0_ref,
                         o_ref, s_out_ref, ext_ref, cols_ref, o_scr):
    steps = x_ref.shape[0]
    ext_ref[0:CONV_W - 1, :] = buf_ref[...]
    ext_ref[CONV_W - 1:CONV_W - 1 + steps, :] = x_ref[...]
    y = ext_ref[0:steps, :] * cw_ref[0:1, :]
    for j in range(1, CONV_W):
        y = y + ext_ref[j:j + steps, :] * cw_ref[j:j + 1, :]
    y = _silu(y)
    ab = ab_ref[...]
    eg = jnp.exp(-jnp.exp(alog_ref[...]) * jax.nn.softplus(ab + dtb_ref[...]))
    beta = jax.nn.sigmoid(ab)

    cols_ref[...] = jnp.zeros_like(cols_ref)
    vs = []
    for h in range(HEADS):
        qh = y[:, h * DH:(h + 1) * DH]
        kh = y[:, WIDTH + h * DH:WIDTH + (h + 1) * DH]
        qh = qh * lax.rsqrt(jnp.sum(qh * qh, axis=-1, keepdims=True) + NORM_EPS) * (DH ** -0.5)
        kh = kh * lax.rsqrt(jnp.sum(kh * kh, axis=-1, keepdims=True) + NORM_EPS)
        cols_ref[h * steps:(h + 1) * steps, :] = kh
        cols_ref[(HEADS + h) * steps:(HEADS + h + 1) * steps, :] = qh
        vs.append(y[:, 2 * WIDTH + h * DH:2 * WIDTH + (h + 1) * DH])
    cols = cols_ref[...].T

    for h in range(HEADS):
        s = s0_ref[h]
        for t in range(steps):
            kcol = cols[:, h * steps + t:h * steps + t + 1]
            qcol = cols[:, (HEADS + h) * steps + t:(HEADS + h) * steps + t + 1]
            s = s * eg[t:t + 1, h:h + 1]
            ks = jnp.sum(s * kcol, axis=0, keepdims=True)
            delta = beta[t:t + 1, SUBLANES + h:SUBLANES + h + 1] * (vs[h][t:t + 1, :] - ks)
            s = s + kcol * delta
            o_scr[t:t + 1, h * DH:(h + 1) * DH] = jnp.sum(s * qcol, axis=0, keepdims=True)
        s_out_ref[h] = s
    o_ref[...] = o_scr[0:steps, :]


def _delta_sample(qkv_pre, ab, conv_buf, conv_w, alog_row, dtb_row, state, layer, nb, steps):
    ncols = 3 * WIDTH
    xr = qkv_pre.reshape(nb, steps, ncols)
    abr = ab.reshape(nb, steps, LANES)
    return pl.pallas_call(
        _delta_sample_kernel,
        grid=(nb,),
        in_specs=[pl.BlockSpec((None, steps, ncols), lambda i: (i, 0, 0)),
                  pl.BlockSpec((None, None, CONV_W - 1, ncols), lambda i: (layer, i, 0, 0)),
                  pl.BlockSpec((None, steps, LANES), lambda i: (i, 0, 0)),
                  pl.BlockSpec((CONV_W, ncols), lambda i: (0, 0)),
                  pl.BlockSpec((1, LANES), lambda i: (0, 0)),
                  pl.BlockSpec((1, LANES), lambda i: (0, 0)),
                  pl.BlockSpec((None, None, HEADS, DH, DH), lambda i: (layer, i, 0, 0, 0))],
        out_specs=[pl.BlockSpec((None, steps, WIDTH), lambda i: (i, 0, 0)),
                   pl.BlockSpec((None, HEADS, DH, DH), lambda i: (i, 0, 0, 0))],
        out_shape=[jax.ShapeDtypeStruct((nb, steps, WIDTH), F32),
                   jax.ShapeDtypeStruct((nb, HEADS, DH, DH), F32)],
        scratch_shapes=[pltpu.VMEM((2 * SUBLANES, ncols), F32),
                        pltpu.VMEM((DH, DH), F32),
                        pltpu.VMEM((SUBLANES, WIDTH), F32)],
        compiler_params=_params(1),
        name="delta_sample",
    )(xr, conv_buf, abr, conv_w, alog_row, dtb_row, state)


def _attn_prompt_kernel(q_ref, kc_ref, kp_ref, vc_ref, vp_ref, o_ref, lse_ref):
    lt = q_ref.shape[0]
    li = pl.program_id(2)
    row = lax.broadcasted_iota(I32, (KEYS_BACK, KEYS_BACK), 0)
    col = lax.broadcasted_iota(I32, (KEYS_BACK, KEYS_BACK), 1)
    mask_cur = col <= row
    mask_prev = col >= row
    neg = -jnp.inf
    for blk in range(lt // KEYS_BACK):
        rs = slice(blk * KEYS_BACK, (blk + 1) * KEYS_BACK)
        lse_tile = jnp.zeros((KEYS_BACK, LANES), F32)
        for h in range(HEADS):
            sl = slice(h * DH, (h + 1) * DH)
            qb = q_ref[rs, sl]
            if blk == 0:
                kprev, vprev = kp_ref[:, sl], vp_ref[:, sl]
                mprev = jnp.logical_and(mask_prev, li > 0)
            else:
                ps = slice((blk - 1) * KEYS_BACK, blk * KEYS_BACK)
                kprev, vprev = kc_ref[ps, sl], vc_ref[ps, sl]
                mprev = mask_prev
            s_c = jnp.where(mask_cur, _dot_nt(qb, kc_ref[rs, sl]), neg)
            s_p = jnp.where(mprev, _dot_nt(qb, kprev), neg)
            m = jnp.maximum(jnp.max(s_c, axis=-1, keepdims=True), jnp.max(s_p, axis=-1, keepdims=True))
            p_c = jnp.exp(s_c - m)
            p_p = jnp.exp(s_p - m)
            den = jnp.sum(p_c, axis=-1, keepdims=True) + jnp.sum(p_p, axis=-1, keepdims=True)
            o = (_dot(p_c.astype(BF16), vc_ref[rs, sl]) + _dot(p_p.astype(BF16), vprev)) / den
            o_ref[rs, sl] = o.astype(o_ref.dtype)
            lse_tile = jnp.where(col == h, m + jnp.log(den), lse_tile)
        lse_ref[rs, :] = lse_tile


def _attn_prompt(q, k, v, g):
    b, dil, ln, _ = q.shape
    lt = min(ROW_TILE, ln)
    nb = lt // KEYS_BACK
    cur = pl.BlockSpec((None, None, lt, WIDTH), lambda bi, r, li: (bi, r, li, 0))
    prev = pl.BlockSpec((None, None, KEYS_BACK, WIDTH),
                        lambda bi, r, li: (bi, r, jnp.maximum(li * nb - 1, 0), 0))
    return pl.pallas_call(
        _attn_prompt_kernel,
        grid=(b, dil, ln // lt),
        in_specs=[cur, cur, prev, cur, prev],
        out_specs=[pl.BlockSpec((None, lt, WIDTH), lambda bi, r, li: (bi, li, r)),
                   pl.BlockSpec((None, lt, LANES), lambda bi, r, li: (bi, li, r))],
        out_shape=[jax.ShapeDtypeStruct((b, ln, dil * WIDTH), BF16),
                   jax.ShapeDtypeStruct((b, ln, dil * LANES), F32)],
        compiler_params=_params(3),
        name=f"attn_prompt_g{g}",
    )(q, k, k, v, v)


def _lane_sums(x):
    shp = x.shape
    flat = x.reshape(-1, LANES).astype(BF16)
    return _dot(flat, jnp.ones((LANES, LANES), BF16)).reshape(shp)


def _attn_sample_kernel(q_ref, k_ref, v_ref, c0_ref, c1_ref, c2_ref, o_ref):
    steps = q_ref.shape[1]
    outs, lses = [], []

    q0, k0, v0 = q_ref[0], k_ref[0], v_ref[0]
    kc, vc = c0_ref[:, 0], c0_ref[:, 1]
    ridx = lax.broadcasted_iota(I32, kc.shape, 0)
    o_rows, l_rows = [], []
    for s in range(steps):
        sc = jnp.where(ridx >= s, _lane_sums(kc * q0[s:s + 1]), -jnp.inf)
        sn = _lane_sums(k0[0:s + 1] * q0[s:s + 1])
        m = jnp.maximum(jnp.max(sc, axis=0, keepdims=True), jnp.max(sn, axis=0, keepdims=True))
        p = jnp.exp(sc - m)
        pn = jnp.exp(sn - m)
        den = jnp.sum(p, axis=0, keepdims=True) + jnp.sum(pn, axis=0, keepdims=True)
        num = jnp.sum(p * vc, axis=0, keepdims=True) + jnp.sum(pn * v0[0:s + 1], axis=0, keepdims=True)
        o_rows.append(num / den)
        l_rows.append(m + jnp.log(den))
    outs.append(jnp.concatenate(o_rows, axis=0))
    lses.append(jnp.concatenate(l_rows, axis=0))

    for gi, cref in ((1, c1_ref), (2, c2_ref)):
        qg, kg, vg = q_ref[gi], k_ref[gi], v_ref[gi]
        if gi == 1:
            dil = GROUPS[1][1]
            nrow = cref.shape[0] // dil
            kc = cref[:, 0].reshape(nrow, dil, HEADS, DH)[:, :steps]
            vc = cref[:, 1].reshape(nrow, dil, HEADS, DH)[:, :steps]
        else:
            kc, vc = cref[:, :, 0], cref[:, :, 1]
        sc = _lane_sums(kc * qg[None])
        sn = _lane_sums(kg * qg)
        m = jnp.maximum(jnp.max(sc, axis=0), sn)
        p = jnp.exp(sc - m[None])
        pn = jnp.exp(sn - m)
        den = jnp.sum(p, axis=0) + pn
        outs.append((jnp.sum(p * vc, axis=0) + pn * vg) / den)
        lses.append(m + jnp.log(den))

    m = jnp.maximum(jnp.maximum(lses[0], lses[1]), lses[2])
    es = [jnp.exp(l - m) for l in lses]
    den = es[0] + es[1] + es[2]
    o_ref[...] = (es[0] * outs[0] + es[1] * outs[1] + es[2] * outs[2]) / den


def _attn_sample(q, k, v, cache0, cache1, cache2, layer):
    nb, _, steps, _, _ = q.shape
    new = pl.BlockSpec((None, 3, steps, HEADS, DH), lambda i: (i, 0, 0, 0, 0))
    w0, w1 = cache0.shape[2], cache1.shape[2]
    dil2 = GROUPS[2][1]
    c2 = cache2.reshape(cache2.shape[0], nb, cache2.shape[2] // dil2, dil2, 2, HEADS, DH)
    return pl.pallas_call(
        _attn_sample_kernel,
        grid=(nb,),
        in_specs=[new, new, new,
                  pl.BlockSpec((None, None, w0, 2, HEADS, DH), lambda i: (layer, i, 0, 0, 0, 0)),
                  pl.BlockSpec((None, None, w1, 2, HEADS, DH), lambda i: (layer, i, 0, 0, 0, 0)),
                  pl.BlockSpec((None, None, c2.shape[2], steps, 2, HEADS, DH),
                               lambda i: (layer, i, 0, 0, 0, 0, 0))],
        out_specs=pl.BlockSpec((None, steps, HEADS, DH), lambda i: (i, 0, 0, 0)),
        out_shape=jax.ShapeDtypeStruct((nb, steps, HEADS, DH), F32),
        compiler_params=_params(1),
        name="attn_sample",
    )(q, k, v, cache0, cache1, c2)


def _pro_delta(refs, is_sample, y_ref):
    op_ref, zp_ref, os_ref, zs_ref, og_ref = refs
    og = og_ref[...]
    for h in range(HEADS):
        sl = slice(h * DH, (h + 1) * DH)
        oh = jnp.where(is_sample, os_ref[:, sl], op_ref[:, sl])
        zh = jnp.where(is_sample, zs_ref[:, sl], zp_ref[:, sl]).astype(F32)
        oh = oh * lax.rsqrt(jnp.mean(oh * oh, axis=-1, keepdims=True) + NORM_EPS) * og
        y_ref[:, sl] = (oh * _silu(zh)).astype(BF16)


def _pro_attn(refs, is_sample, y_ref):
    o0, o1, o2, l0, l1, l2, ys_ref = refs
    a, b, c = l0[...], l1[...], l2[...]
    m = jnp.maximum(jnp.maximum(a, b), c)
    ea, eb, ec = jnp.exp(a - m), jnp.exp(b - m), jnp.exp(c - m)
    den = ea + eb + ec
    wa, wb, wc = ea / den, eb / den, ec / den
    for h in range(HEADS):
        sl = slice(h * DH, (h + 1) * DH)
        y = (wa[:, h:h + 1] * o0[:, sl].astype(F32) + wb[:, h:h + 1] * o1[:, sl].astype(F32)
             + wc[:, h:h + 1] * o2[:, sl].astype(F32))
        y_ref[:, sl] = jnp.where(is_sample, ys_ref[:, sl], y).astype(BF16)


def _route(logits_t):
    rows = [logits_t[e:e + 1, :] for e in range(N_EXPERTS)]
    mx = rows[0]
    for r in rows[1:]:
        mx = jnp.maximum(mx, r)
    ex = [jnp.exp(r - mx) for r in rows]
    gs = []
    for g in range(N_EXPERT_GROUPS):
        a, b, c, d = ex[4 * g:4 * g + 4]
        gs.append(jnp.maximum(jnp.maximum(jnp.maximum(a + b, a + c), jnp.maximum(a + d, b + c)),
                              jnp.maximum(b + d, c + d)))
    best = jnp.maximum(jnp.maximum(gs[0], gs[1]), jnp.maximum(gs[2], gs[3]))
    taken = jnp.zeros_like(best) > 1.0
    gsel = jnp.zeros_like(best)
    p = [jnp.zeros_like(best) for _ in range(EXPERTS_PER_GROUP)]
    for g in range(N_EXPERT_GROUPS):
        here = jnp.logical_and(jnp.logical_not(taken), gs[g] == best)
        taken = jnp.logical_or(taken, here)
        gsel = jnp.where(here, float(g), gsel)
        for k in range(EXPERTS_PER_GROUP):
            p[k] = jnp.where(here, ex[4 * g + k], p[k])

    def first_max(vals):
        top = jnp.maximum(jnp.maximum(vals[0], vals[1]), jnp.maximum(vals[2], vals[3]))
        found = jnp.zeros_like(top) > 1.0
        idx = jnp.zeros_like(top)
        for k in range(EXPERTS_PER_GROUP):
            here = jnp.logical_and(jnp.logical_not(found), vals[k] == top)
            found = jnp.logical_or(found, here)
            idx = jnp.where(here, float(k), idx)
        return top, idx

    v1, i1 = first_max(p)
    rest = [jnp.where(i1 == float(k), -1.0, p[k]) for k in range(EXPERTS_PER_GROUP)]
    v2, i2 = first_max(rest)
    lo = jnp.minimum(i1, i2)
    hi = jnp.maximum(i1, i2)
    p_lo = jnp.where(i1 < i2, v1, v2)
    p_hi = jnp.where(i1 < i2, v2, v1)
    pair = lo * (7.0 - lo) * 0.5 + hi - lo - 1.0
    visit = jnp.zeros_like(pair)
    for rank, (pidx, _, _, _) in enumerate(VISIT):
        visit = jnp.where(pair == float(pidx), float(rank), visit)
    cls = gsel * float(len(PAIRS)) + visit
    tot = p_lo + p_hi
    return cls, p_lo / tot, p_hi / tot


def _piece_copy(src_ref, src_row, dst_ref, dst_row, sem):
    return pltpu.make_async_copy(src_ref.at[pl.ds(pl.multiple_of(src_row, SUBLANES), SUBLANES), :],
                                 dst_ref.at[pl.ds(pl.multiple_of(dst_row, SUBLANES), SUBLANES), :], sem)


def _wait_pieces(src_ref, dst_ref, sem, n):
    def body(p, carry):
        _piece_copy(src_ref, 0, dst_ref, 0, sem).wait()
        return carry

    lax.fori_loop(0, n, body, 0)


def _post_kernel(*refs, n_pro, prologue, n_prompt_tiles):
    pro = refs[:n_pro]
    (x_ref, wo_ref, g1q_ref, scq_ref, shq_ref, g1r_ref, scr_ref, shr_ref, nf_ref, rw_ref, rb_ref,
     x1_ref, rt_ref, xsort_ref, off_ref, len_ref, cnt_ref,
     wob_ref, y_ref, srt_ref, zero_ref, cnt_v, cnt_s, run_ref, pend_ref, sems, sem_s) = refs[n_pro:]
    i = pl.program_id(0)
    n_steps = pl.num_programs(0)
    tm = x_ref.shape[0]
    is_sample = i >= n_prompt_tiles
    slot = i % 2

    @pl.when(i == 0)
    def _():
        _cast_rows(wob_ref, wo_ref)
        zero_ref[...] = jnp.zeros_like(zero_ref)
        for c in range(CLASS_SLOTS):
            run_ref[c] = 0
        pend_ref[0] = 0
        pend_ref[1] = 0

    prologue(pro, is_sample, y_ref)
    g1 = jnp.where(is_sample, g1r_ref[...], g1q_ref[...])
    sc = jnp.where(is_sample, scr_ref[...], scq_ref[...])
    sh = jnp.where(is_sample, shr_ref[...], shq_ref[...])
    x1 = x_ref[...] + g1 * _dot(y_ref[...], wob_ref[...])
    x1_ref[...] = x1
    h2 = _norm_mod(x1, nf_ref[...], sc, sh)
    logits = _dot_2x2(h2, rw_ref[...]) + rb_ref[...]
    cls, w_lo, w_hi = _route(logits.T)

    cls_i = cls.astype(I32)
    cid = lax.broadcasted_iota(I32, (LANES, tm), 0)
    oh = jnp.where(cid == cls_i, 1.0, 0.0)
    rr = lax.broadcasted_iota(I32, (tm, tm), 0)
    cc = lax.broadcasted_iota(I32, (tm, tm), 1)
    incl = jnp.where(rr <= cc, 1.0, 0.0).astype(BF16)
    pre = _dot(oh.astype(BF16), incl)
    rank = jnp.sum(oh * pre, axis=0, keepdims=True) - 1.0
    cnt_col = jnp.sum(oh, axis=1, keepdims=True)
    pad_col = jnp.floor((cnt_col + (SUBLANES - 1)) * (1.0 / SUBLANES)) * SUBLANES
    r128 = lax.broadcasted_iota(I32, (LANES, LANES), 0)
    c128 = lax.broadcasted_iota(I32, (LANES, LANES), 1)
    below = jnp.where(c128 < r128, 1.0, 0.0).astype(BF16)
    start_col = _dot(below, jnp.broadcast_to(pad_col, (LANES, LANES)).astype(BF16))[:, 0:1]
    dest = rank + jnp.sum(oh * start_col, axis=0, keepdims=True)

    route8 = jnp.concatenate([cls, w_lo, w_hi, dest, jnp.zeros((LANES - 4, tm), F32)], axis=0)
    rt = route8.T
    rt_ref[...] = rt

    lane = lax.broadcasted_iota(I32, (tm, LANES), 1)
    oh_t = jnp.where(lane == rt[:, 0:1].astype(I32), 1.0, 0.0)
    cnt_row = jnp.sum(oh_t, axis=0, keepdims=True)
    pad_row = jnp.floor((cnt_row + (SUBLANES - 1)) * (1.0 / SUBLANES)) * SUBLANES
    above = jnp.where(r128 < c128, 1.0, 0.0).astype(BF16)
    start_row = _dot(jnp.broadcast_to(pad_row, (SUBLANES, LANES)).astype(BF16), above)[0:1, :]
    cnt_v[0:1, :] = pad_row.astype(I32)
    cnt_v[1:2, :] = start_row.astype(I32)
    to_smem = pltpu.make_async_copy(cnt_v, cnt_s, sem_s)
    to_smem.start()

    dest_i = dest.astype(I32)
    perm = jnp.where(lax.broadcasted_iota(I32, (SORT_ROWS, tm), 0) == dest_i, 1.0, 0.0).astype(BF16)
    r1 = rt.astype(BF16)
    r2 = (rt - r1.astype(F32)).astype(BF16)
    r3 = (rt - r1.astype(F32) - r2.astype(F32)).astype(BF16)
    srt_ref[slot, :, 0:WIDTH] = _dot(perm, h2.astype(BF16))
    srt_ref[slot, :, WIDTH:SORT_COLS] = _dot(perm, r1) + _dot(perm, r2) + _dot(perm, r3)
    to_smem.wait()

    src = srt_ref.at[slot]
    total = 0
    for c in range(N_CLASSES):
        n_rows = cnt_s[0, c]
        src0 = cnt_s[1, c]
        dst0 = run_ref[c]
        dst = xsort_ref.at[c]

        def issue(p, carry, src0=src0, dst0=dst0, dst=dst):
            _piece_copy(src, src0 + p * SUBLANES, dst, dst0 + p * SUBLANES, sems.at[slot]).start()
            return carry

        n_pieces = n_rows // SUBLANES
        lax.fori_loop(0, n_pieces, issue, 0)
        off_ref[i * CLASS_SLOTS + c] = dst0
        len_ref[i * CLASS_SLOTS + c] = n_rows
        run_ref[c] = dst0 + n_rows
        total = total + n_pieces
    for c in range(N_CLASSES, CLASS_SLOTS):
        off_ref[i * CLASS_SLOTS + c] = 0
        len_ref[i * CLASS_SLOTS + c] = 0
    pend_ref[slot] = total

    @pl.when(i > 0)
    def _():
        _wait_pieces(srt_ref.at[1 - slot], xsort_ref.at[0], sems.at[1 - slot], pend_ref[1 - slot])

    @pl.when(i == n_steps - 1)
    def _():
        _wait_pieces(src, xsort_ref.at[0], sems.at[slot], total)
        n_zero = 0
        for c in range(N_CLASSES):
            used = run_ref[c]
            cnt_ref[c] = used
            dst = xsort_ref.at[c]
            n_fill = (((used + FFN_TILE - 1) // FFN_TILE) * FFN_TILE - used) // SUBLANES

            def fill(p, carry, used=used, dst=dst):
                _piece_copy(zero_ref, 0, dst, used + p * SUBLANES, sems.at[slot]).start()
                return carry

            lax.fori_loop(0, n_fill, fill, 0)
            n_zero = n_zero + n_fill
        for c in range(N_CLASSES, CLASS_SLOTS):
            cnt_ref[c] = 0
        _wait_pieces(zero_ref, xsort_ref.at[0], sems.at[slot], n_zero)


def _post(pro_arrays, pro_specs, prologue, xall, w_out, mod_seq, mod_rows, norm_ffn, rw_pad, rb_pad,
          n_prompt_tiles, tiles_per_seq, cap):
    n, d = xall.shape
    tm = ROW_TILE
    n_tiles = n // tm
    n_seq = mod_seq.shape[0]
    kernel = functools.partial(_post_kernel, n_pro=len(pro_arrays), prologue=prologue,
                               n_prompt_tiles=n_prompt_tiles)
    const = lambda i: (0, 0)

    def seq_spec(k):
        return pl.BlockSpec((None, 1, d), lambda i: (jnp.minimum(i // tiles_per_seq, n_seq - 1), 0, k))

    def row_spec(k):
        return pl.BlockSpec((tm, d), lambda i: (jnp.maximum(i - n_prompt_tiles, 0), k))

    smem = pl.BlockSpec(memory_space=pltpu.SMEM)
    return pl.pallas_call(
        kernel,
        grid=(n_tiles,),
        in_specs=list(pro_specs) + [
            pl.BlockSpec((tm, d), lambda i: (i, 0)),
            pl.BlockSpec((WIDTH, d), const),
            seq_spec(2), seq_spec(4), seq_spec(3),
            row_spec(2), row_spec(4), row_spec(3),
            pl.BlockSpec((1, d), const),
            pl.BlockSpec((d, LANES), const),
            pl.BlockSpec((1, LANES), const)],
        out_specs=[pl.BlockSpec((tm, d), lambda i: (i, 0)),
                   pl.BlockSpec((tm, LANES), lambda i: (i, 0)),
                   pl.BlockSpec(memory_space=pl.ANY),
                   smem, smem, smem],
        out_shape=[jax.ShapeDtypeStruct((n, d), F32),
                   jax.ShapeDtypeStruct((n, LANES), F32),
                   jax.ShapeDtypeStruct((N_CLASSES, cap, SORT_COLS), F32),
                   jax.ShapeDtypeStruct((n_tiles * CLASS_SLOTS,), I32),
                   jax.ShapeDtypeStruct((n_tiles * CLASS_SLOTS,), I32),
                   jax.ShapeDtypeStruct((CLASS_SLOTS,), I32)],
        scratch_shapes=[pltpu.VMEM((WIDTH, d), BF16),
                        pltpu.VMEM((tm, WIDTH), BF16),
                        pltpu.VMEM((2, SORT_ROWS, SORT_COLS), F32),
                        pltpu.VMEM((SUBLANES, SORT_COLS), F32),
                        pltpu.VMEM((SUBLANES, LANES), I32),
                        pltpu.SMEM((SUBLANES, LANES), I32),
                        pltpu.SMEM((CLASS_SLOTS,), I32),
                        pltpu.SMEM((2,), I32),
                        pltpu.SemaphoreType.DMA((2,)),
                        pltpu.SemaphoreType.DMA(())],
        compiler_params=_params(1),
        name="post",
    )(*pro_arrays, xall, w_out, mod_seq, mod_seq, mod_seq, mod_rows, mod_rows, mod_rows,
      norm_ffn, rw_pad, rb_pad)


def _ffn_kernel(tc_ref, tl_ref, ea_ref, eb_ref, swap_ref, ca_ref, cb_ref, nu_ref,
                x_ref, ga_ref, ua_ref, da_ref, gb_ref, ub_ref, db_ref,
                o_ref, wa_gu, wa_d, wb_gu, wb_d):
    j = pl.program_id(0)
    f = ga_ref.shape[1]

    @pl.when(ca_ref[j] == 1)
    def _():
        _cast_rows(wa_gu, ga_ref, 0)
        _cast_rows(wa_gu, ua_ref, f)
        _cast_rows(wa_d, da_ref)

    @pl.when(cb_ref[j] == 1)
    def _():
        _cast_rows(wb_gu, gb_ref, 0)
        _cast_rows(wb_gu, ub_ref, f)
        _cast_rows(wb_d, db_ref)

    @pl.when(j < nu_ref[0])
    def _():
        x = x_ref[:, 0:WIDTH].astype(BF16)
        w_lo = x_ref[:, WIDTH + 1:WIDTH + 2]
        w_hi = x_ref[:, WIDTH + 2:WIDTH + 3]
        swapped = swap_ref[j] == 1
        w_a = jnp.where(swapped, w_hi, w_lo)
        w_b = jnp.where(swapped, w_lo, w_hi)
        ha = _dot(x, wa_gu[...])
        act_a = (_silu(ha[:, :f]) * ha[:, f:] * w_a).astype(BF16)
        hb = _dot(x, wb_gu[...])
        act_b = (_silu(hb[:, :f]) * hb[:, f:] * w_b).astype(BF16)
        o_ref[...] = _dot(act_a, wa_d[...]) + _dot(act_b, wb_d[...])


def _ffn(xsort, w_gate, w_up, w_down, layer, tables, n_ffn_tiles):
    ncls, cap, _ = xsort.shape
    d = w_gate.shape[2]
    f = w_gate.shape[3]
    tm = FFN_TILE
    wa_in = pl.BlockSpec((None, None, d, f), lambda j, tc, tl, ea, eb, sw, ca, cb, nu: (layer, ea[j], 0, 0))
    wa_dn = pl.BlockSpec((None, None, f, d), lambda j, tc, tl, ea, eb, sw, ca, cb, nu: (layer, ea[j], 0, 0))
    wb_in = pl.BlockSpec((None, None, d, f), lambda j, tc, tl, ea, eb, sw, ca, cb, nu: (layer, eb[j], 0, 0))
    wb_dn = pl.BlockSpec((None, None, f, d), lambda j, tc, tl, ea, eb, sw, ca, cb, nu: (layer, eb[j], 0, 0))
    rows = lambda j, tc, tl, ea, eb, sw, ca, cb, nu: (tc[j], tl[j], 0)
    return pl.pallas_call(
        _ffn_kernel,
        grid_spec=pltpu.PrefetchScalarGridSpec(
            num_scalar_prefetch=8,
            grid=(n_ffn_tiles,),
            in_specs=[pl.BlockSpec((None, tm, SORT_COLS), rows),
                      wa_in, wa_in, wa_dn, wb_in, wb_in, wb_dn],
            out_specs=pl.BlockSpec((None, tm, d), rows),
            scratch_shapes=[pltpu.VMEM((d, 2 * f), BF16), pltpu.VMEM((f, d), BF16),
                            pltpu.VMEM((d, 2 * f), BF16), pltpu.VMEM((f, d), BF16)]),
        out_shape=jax.ShapeDtypeStruct((ncls, cap, d), F32),
        compiler_params=_params(1),
        name="moe_ffn",
    )(*tables, xsort, w_gate, w_up, w_down, w_gate, w_up, w_down)


def _ffn_tables(counts, n_ffn_tiles):
    cnt = counts[:N_CLASSES]
    tiles = (cnt + FFN_TILE - 1) // FFN_TILE
    ends = jnp.cumsum(tiles)
    nu = ends[-1]
    jj = jnp.minimum(jnp.arange(n_ffn_tiles, dtype=I32), jnp.maximum(nu - 1, 0))
    tcls = jnp.sum((ends[None, :] <= jj[:, None]).astype(I32), axis=1)
    tcls = jnp.minimum(tcls, N_CLASSES - 1)
    tloc = jj - (ends - tiles)[tcls]
    npair = len(PAIRS)
    vtab = jnp.asarray(VISIT, I32)
    grp = tcls // npair
    vis = tcls % npair
    ea = grp * EXPERTS_PER_GROUP + vtab[vis, 1]
    eb = grp * EXPERTS_PER_GROUP + vtab[vis, 2]
    swap = vtab[vis, 3]
    one = jnp.ones((1,), I32)
    ca = jnp.concatenate([one, (ea[1:] != ea[:-1]).astype(I32)])
    cb = jnp.concatenate([one, (eb[1:] != eb[:-1]).astype(I32)])
    return (tcls.astype(I32), tloc.astype(I32), ea.astype(I32), eb.astype(I32), swap.astype(I32),
            ca, cb, nu.astype(I32).reshape(1))


def _combine_kernel(off_ref, len_ref, y_ref, x_ref, rt_ref, g2q_ref, g2r_ref, gain_ref, o_ref, seg_ref, sem,
                    *, final, n_prompt_tiles):
    i = pl.program_id(0)
    tm = o_ref.shape[0]

    @pl.when(i == 0)
    def _():
        seg_ref[...] = jnp.zeros_like(seg_ref)

    start = 0
    total = 0
    for c in range(N_CLASSES):
        n_rows = len_ref[i * CLASS_SLOTS + c]
        src0 = off_ref[i * CLASS_SLOTS + c]
        src = y_ref.at[c]

        def issue(p, carry, src=src, src0=src0, start=start):
            _piece_copy(src, src0 + p * SUBLANES, seg_ref, start + p * SUBLANES, sem).start()
            return carry

        n_pieces = n_rows // SUBLANES
        lax.fori_loop(0, n_pieces, issue, 0)
        start = start + n_rows
        total = total + n_pieces
    _wait_pieces(y_ref.at[0], seg_ref, sem, total)

    dest = rt_ref[:, 3:4].astype(I32)
    unperm = jnp.where(lax.broadcasted_iota(I32, (tm, SORT_ROWS), 1) == dest, 1.0, 0.0).astype(BF16)
    y = _dot(unperm, seg_ref[...].astype(BF16))
    g2 = jnp.where(i >= n_prompt_tiles, g2r_ref[...], g2q_ref[...])
    x2 = x_ref[...] + g2 * y
    if final:
        ms = jnp.mean(x2 * x2, axis=-1, keepdims=True)
        x2 = x2 * lax.rsqrt(ms + NORM_EPS) * gain_ref[...]
    o_ref[...] = x2


def _combine(y_sorted, off, length, x1, rt, mod_seq, mod_rows, gain, final, n_prompt_tiles, tiles_per_seq):
    n, d = x1.shape
    tm = ROW_TILE
    n_seq = mod_seq.shape[0]
    kernel = functools.partial(_combine_kernel, final=final, n_prompt_tiles=n_prompt_tiles)
    return pl.pallas_call(
        kernel,
        grid_spec=pltpu.PrefetchScalarGridSpec(
            num_scalar_prefetch=2,
            grid=(n // tm,),
            in_specs=[pl.BlockSpec(memory_space=pl.ANY),
                      pl.BlockSpec((tm, d), lambda i, o, l: (i, 0)),
                      pl.BlockSpec((tm, LANES), lambda i, o, l: (i, 0)),
                      pl.BlockSpec((None, 1, d),
                                   lambda i, o, l: (jnp.minimum(i // tiles_per_seq, n_seq - 1), 0, 5)),
                      pl.BlockSpec((tm, d), lambda i, o, l: (jnp.maximum(i - n_prompt_tiles, 0), 5)),
                      pl.BlockSpec((1, d), lambda i, o, l: (0, 0))],
            out_specs=pl.BlockSpec((tm, d), lambda i, o, l: (i, 0)),
            scratch_shapes=[pltpu.VMEM((SORT_ROWS, d), F32), pltpu.SemaphoreType.DMA(())]),
        out_shape=jax.ShapeDtypeStruct((n, d), F32),
        compiler_params=_params(1),
        name="combine",
    )(off, length, y_sorted, x1, rt, mod_seq, mod_rows, gain)


def kernel(x_prompt, x_sample, state_conv, state_delta, cache_kv0, cache_kv1, cache_kv2, c_prompt, c_sample,
           w_ada, b_ada, norm_mix, norm_ffn, norm_final, a_w_in, a_conv, a_log, a_dt_bias, a_out_norm, a_w_out,
           b_w_in, b_w_out, router_w, router_b, exp_w_gate, exp_w_up, exp_w_down):
    b, t, d = x_prompt.shape
    nb, steps, _ = x_sample.shape
    depth = w_ada.shape[0]
    n_p, n_s = b * t, nb * steps
    n_all = n_p + n_s
    assert d == WIDTH and t % ROW_TILE == 0 and n_s % ROW_TILE == 0 and t % MIX_TILE == 0
    assert steps <= CONV_W and cache_kv0.shape[2] == GROUPS[0][0] and cache_kv1.shape[2] == GROUPS[1][0]
    assert cache_kv2.shape[2] == GROUPS[2][0] and t % (GROUPS[2][1] * KEYS_BACK) == 0
    tiles_per_seq = t // ROW_TILE
    n_pt, n_st = n_p // ROW_TILE, n_s // ROW_TILE
    n_tiles = n_pt + n_st
    tm = ROW_TILE

    rows_c = b + nb
    rows_pad = -(-rows_c // SUBLANES) * SUBLANES
    c_all = jnp.concatenate([c_prompt, c_sample, jnp.zeros((rows_pad - rows_c, d), F32)], axis=0)
    mod = _adaln(c_all, w_ada, b_ada)
    mod_p = mod[:, :b].reshape(depth, b, 1, 6 * d)
    mod_s = jnp.repeat(mod[:, b:b + nb], steps, axis=1)

    rw_pad = jnp.pad(router_w, ((0, 0), (0, LANES - N_EXPERTS)))
    rb_pad = jnp.pad(router_b, (0, LANES - N_EXPERTS), constant_values=-1e30).reshape(1, LANES)
    tabs_p = _rope_tables(jnp.arange(t, dtype=I32))
    past = cache_kv2.shape[2]
    tabs_s = _rope_tables(past + (jnp.arange(n_s, dtype=I32) % steps))

    cap = -(-(n_all + n_tiles * (SUBLANES - 1)) // FFN_TILE) * FFN_TILE
    n_ffn_tiles = -(-(n_all + n_tiles * N_CLASSES * (SUBLANES - 1)) // FFN_TILE) + N_CLASSES

    xall = jnp.concatenate([x_prompt.reshape(n_p, d), x_sample.reshape(n_s, d)], axis=0)
    convs_p, deltas_p, convs_s, deltas_s = [], [], [], []
    kvs_p = [[] for _ in GROUPS]
    kvs_s = [[] for _ in GROUPS]

    def prompt_rows(width):
        return pl.BlockSpec((tm, width), lambda r: (jnp.minimum(r, n_pt - 1), 0))

    def sample_rows(width):
        return pl.BlockSpec((tm, width), lambda r: (jnp.maximum(r - n_pt, 0), 0))

    for i in range(depth):
        j = i // N_MIXERS
        gain_mix = norm_mix[i].reshape(1, d)
        gain_ffn = norm_ffn[i].reshape(1, d)
        mp, ms = mod_p[i], mod_s[i]
        sc_p = pl.BlockSpec((None, 1, d), lambda r: (r // tiles_per_seq, 0, 1))
        sh_p = pl.BlockSpec((None, 1, d), lambda r: (r // tiles_per_seq, 0, 0))
        sc_s = pl.BlockSpec((tm, d), lambda r: (r, 1))
        sh_s = pl.BlockSpec((tm, d), lambda r: (r, 0))

        if i % N_MIXERS == 0:
            ncols = 3 * WIDTH
            w_qkv = a_w_in[j]
            zab_cols = WIDTH + LANES
            w_zab = jnp.pad(a_w_in[j][:, ncols:], ((0, 0), (0, zab_cols - (a_w_in.shape[2] - ncols))))
            alog_row = jnp.pad(a_log[j], (0, LANES - HEADS)).reshape(1, LANES)
            dtb_row = jnp.pad(a_dt_bias[j], (0, LANES - HEADS)).reshape(1, LANES)
            og_row = a_out_norm[j].reshape(1, DH)
            wq_spec = pl.BlockSpec((d, ncols), lambda r: (0, 0))
            wz_spec = pl.BlockSpec((d, zab_cols), lambda r: (0, 0))

            qkv_p, tail_p = _inproj_rows(
                xall, 0, n_p, gain_mix, mp, mp, sc_p, sh_p, w_qkv, wq_spec, ncols, (), (),
                [jax.ShapeDtypeStruct((n_p, ncols), BF16), jax.ShapeDtypeStruct((b, SUBLANES, ncols), F32)],
                [pl.BlockSpec((tm, ncols), lambda r: (r, 0)),
                 pl.BlockSpec((None, SUBLANES, ncols), lambda r: (r // tiles_per_seq, 0, 0))],
                functools.partial(_epi_delta_qkv, with_tail=True), tm, "inproj_delta_qkv")
            z_p, ab_p = _inproj_rows(
                xall, 0, n_p, gain_mix, mp, mp, sc_p, sh_p, w_zab, wz_spec, zab_cols, (), (),
                [jax.ShapeDtypeStruct((n_p, WIDTH), BF16), jax.ShapeDtypeStruct((n_p, LANES), F32)],
                [pl.BlockSpec((tm, WIDTH), lambda r: (r, 0)),
                 pl.BlockSpec((tm, LANES), lambda r: (r, 0))],
                _epi_delta_zab, tm, "inproj_delta_zab")
            convs_p.append(tail_p[:, SUBLANES - (CONV_W - 1):, :])
            u, w, qd, kdt, aqk, egl = _delta_prep(qkv_p, ab_p, a_conv[j], alog_row, dtb_row, b, t)
            o_p, s_p = _delta_scan(u, w, qd, kdt, aqk, egl, b, t)
            deltas_p.append(s_p)

            qkv_s, = _inproj_rows(
                xall, n_pt, n_s, gain_mix, ms, ms, sc_s, sh_s, w_qkv, wq_spec, ncols, (), (),
                [jax.ShapeDtypeStruct((n_s, ncols), F32)],
                [pl.BlockSpec((tm, ncols), lambda r: (r, 0))],
                functools.partial(_epi_delta_qkv, with_tail=False), tm, "inproj_delta_qkv_s")
            z_s, ab_s = _inproj_rows(
                xall, n_pt, n_s, gain_mix, ms, ms, sc_s, sh_s, w_zab, wz_spec, zab_cols, (), (),
                [jax.ShapeDtypeStruct((n_s, WIDTH), BF16), jax.ShapeDtypeStruct((n_s, LANES), F32)],
                [pl.BlockSpec((tm, WIDTH), lambda r: (r, 0)),
                 pl.BlockSpec((tm, LANES), lambda r: (r, 0))],
                _epi_delta_zab, tm, "inproj_delta_zab_s")
            convs_s.append(qkv_s.reshape(nb, steps, ncols)[:, steps - (CONV_W - 1):, :])
            o_s, s_s = _delta_sample(qkv_s, ab_s, state_conv, a_conv[j], alog_row, dtb_row, state_delta,
                                     j, nb, steps)
            deltas_s.append(s_s)
            pro = ([o_p, z_p, o_s.reshape(n_s, WIDTH), z_s, og_row],
                   [prompt_rows(WIDTH), prompt_rows(WIDTH), sample_rows(WIDTH), sample_rows(WIDTH),
                    pl.BlockSpec((1, DH), lambda r: (0, 0))], _pro_delta)
            w_out = a_w_out[j]
        else:
            x3 = xall[:n_p].reshape(b, t, d)
            os_, ls_ = [], []
            for g, (win, dil) in enumerate(GROUPS):
                q, k, v = _inproj_attn_prompt(x3, gain_mix, mp, b_w_in[j], g, dil, tabs_p)
                o_g, lse_g = _attn_prompt(q, k, v, g)
                os_.append(o_g.reshape(n_p, WIDTH))
                ls_.append(lse_g.reshape(n_p, LANES))
                keep = min(win, t)
                lk = keep // dil
                ln = t // dil

                def tail(a):
                    a = a[:, :, ln - lk:, :]
                    return jnp.transpose(a, (0, 2, 1, 3)).reshape(b, keep, HEADS, DH)

                kvs_p[g].append(jnp.stack([tail(k), tail(v)], axis=2).astype(F32))

            qs, ks, vs = [], [], []
            for g in range(len(GROUPS)):
                outs = _inproj_rows(
                    xall, n_pt, n_s, gain_mix, ms, ms, sc_s, sh_s, b_w_in[j],
                    pl.BlockSpec((d, 3 * WIDTH), lambda r, g=g: (0, g)), 3 * WIDTH,
                    tabs_s, [pl.BlockSpec((tm, DH), lambda r: (r, 0))] * 3,
                    [jax.ShapeDtypeStruct((n_s, WIDTH), F32)] * 3,
                    [pl.BlockSpec((tm, WIDTH), lambda r: (r, 0))] * 3,
                    functools.partial(_epi_attn, out_dtype=F32), tm, f"inproj_attn_s_g{g}")
                qs.append(outs[0].reshape(nb, steps, HEADS, DH))
                ks.append(outs[1].reshape(nb, steps, HEADS, DH))
                vs.append(outs[2].reshape(nb, steps, HEADS, DH))
                kvs_s[g].append(jnp.stack([ks[-1], vs[-1]], axis=2))
            y_attn_s = _attn_sample(jnp.stack(qs, axis=1), jnp.stack(ks, axis=1), jnp.stack(vs, axis=1),
                                    cache_kv0, cache_kv1, cache_kv2, j)
            pro = (os_ + ls_ + [y_attn_s.reshape(n_s, WIDTH)],
                   [prompt_rows(WIDTH)] * 3 + [prompt_rows(LANES)] * 3 + [sample_rows(WIDTH)], _pro_attn)
            w_out = b_w_out[j]

        x1, rt, xsort, off, length, counts = _post(
            pro[0], pro[1], pro[2], xall, w_out, mp, ms, gain_ffn, rw_pad, rb_pad, n_pt, tiles_per_seq, cap)
        tables = _ffn_tables(counts, n_ffn_tiles)
        y_sorted = _ffn(xsort, exp_w_gate, exp_w_up, exp_w_down, i, tables, n_ffn_tiles)
        xall = _combine(y_sorted, off, length, x1, rt, mp, ms, norm_final.reshape(1, d), i == depth - 1,
                        n_pt, tiles_per_seq)

    y_p = xall[:n_p].reshape(b, t, d)
    y_s = xall[n_p:].reshape(nb, steps, d)
    return (y_p, y_s, jnp.stack(convs_p), jnp.stack(deltas_p),
            jnp.stack(kvs_p[0]), jnp.stack(kvs_p[1]), jnp.stack(kvs_p[2]),
            jnp.stack(convs_s), jnp.stack(deltas_s),
            jnp.stack(kvs_s[0]), jnp.stack(kvs_s[1]), jnp.stack(kvs_s[2]))
```

```python
import functools

import jax
import jax.numpy as jnp
from jax import lax
from jax.experimental import pallas as pl
from jax.experimental.pallas import tpu as pltpu

F32 = jnp.float32
BF16 = jnp.bfloat16
I32 = jnp.int32

N_MIXERS = 2
HEADS = 8
DH = 128
WIDTH = HEADS * DH
CONV_W = 4
CHUNK = 64
GROUPS = ((128, 1), (512, 4), (2048, 16))
KEYS_BACK = 128
ROT_DIM = DH // 4
ROPE_THETA = 500000.0
N_EXPERTS = 16
N_EXPERT_GROUPS = 4
EXPERTS_PER_GROUP = 4
NORM_EPS = 1e-6

LANES = 128
SUBLANES = 8
VMEM_LIMIT_BYTES = 58 * 1024 * 1024

ROW_TILE = 512
MIX_TILE = 256
FFN_TILE = 256

PAIRS = ((0, 1), (0, 2), (0, 3), (1, 2), (1, 3), (2, 3))
VISIT = ((0, 0, 1, 0), (3, 2, 1, 1), (5, 2, 3, 0), (2, 0, 3, 0), (1, 0, 2, 0), (4, 1, 3, 0))
N_CLASSES = N_EXPERT_GROUPS * len(PAIRS)
CLASS_SLOTS = 32
SORT_ROWS = ROW_TILE + 2 * LANES
SORT_COLS = WIDTH + LANES
assert SORT_ROWS >= ROW_TILE + N_CLASSES * (SUBLANES - 1)


def _params(n_grid):
    return pltpu.CompilerParams(dimension_semantics=("arbitrary",) * n_grid,
                                vmem_limit_bytes=VMEM_LIMIT_BYTES)


def _first_step(n_grid):
    first = pl.program_id(0) == 0
    for a in range(1, n_grid):
        first = jnp.logical_and(first, pl.program_id(a) == 0)
    return first


def _silu(x):
    return x * jax.nn.sigmoid(x)


def _norm_mod(x, gain, sc, sh):
    ms = jnp.mean(x * x, axis=-1, keepdims=True)
    return (x * lax.rsqrt(ms + NORM_EPS) * gain) * (1.0 + sc) + sh


def _dot(a, b):
    return jnp.dot(a, b, preferred_element_type=F32)


def _dot_nt(a, b):
    return lax.dot_general(a, b, (((1,), (1,)), ((), ())), preferred_element_type=F32)


def _split2(a):
    hi = a.astype(BF16)
    return hi, (a - hi.astype(F32)).astype(BF16)


def _split3(a):
    hi = a.astype(BF16)
    r = a - hi.astype(F32)
    mid = r.astype(BF16)
    return hi, mid, (r - mid.astype(F32)).astype(BF16)


def _dot_2x2(a, b):
    a_hi, a_lo = _split2(a)
    b_hi, b_lo = _split2(b)
    return _dot(a_hi, b_hi) + (_dot(a_lo, b_hi) + _dot(a_hi, b_lo))


def _dot_exact_lhs(a_b, b):
    b_hi, b_mid, b_lo = _split3(b)
    return _dot(a_b, b_hi) + (_dot(a_b, b_mid) + _dot(a_b, b_lo))


def _cast_rows(dst_ref, src_ref, col0=0):
    rows, ncols = src_ref.shape
    step = 128 if rows % 128 == 0 else rows

    def body(c, carry):
        r0 = pl.multiple_of(c * step, step)
        dst_ref[pl.ds(r0, step), col0:col0 + ncols] = src_ref[pl.ds(r0, step), :].astype(BF16)
        return carry

    lax.fori_loop(0, rows // step, body, 0)


def _adaln_kernel(c_ref, w_ref, b_ref, o_ref):
    o_ref[...] = _dot_2x2(_silu(c_ref[...]), w_ref[...]) + b_ref[...]


def _adaln(c_all, w_ada, b_ada):
    rows, d = c_all.shape
    depth, _, six_d = w_ada.shape
    tn = 1024
    return pl.pallas_call(
        _adaln_kernel,
        grid=(depth, six_d // tn),
        in_specs=[pl.BlockSpec((rows, d), lambda l, j: (0, 0)),
                  pl.BlockSpec((None, d, tn), lambda l, j: (l, 0, j)),
                  pl.BlockSpec((None, 1, tn), lambda l, j: (l, 0, j))],
        out_specs=pl.BlockSpec((None, rows, tn), lambda l, j: (l, 0, j)),
        out_shape=jax.ShapeDtypeStruct((depth, rows, six_d), F32),
        compiler_params=_params(2),
        name="adaln",
    )(c_all, w_ada, b_ada.reshape(depth, 1, six_d))


def _inproj_kernel(x_ref, g_ref, sc_ref, sh_ref, w_ref, *rest, n_grid, n_extra, n_out, epilogue):
    extra = rest[:n_extra]
    outs = rest[n_extra:n_extra + n_out]
    wb_ref = rest[n_extra + n_out]

    @pl.when(_first_step(n_grid))
    def _():
        _cast_rows(wb_ref, w_ref)

    h = _norm_mod(x_ref[...], g_ref[...], sc_ref[...], sh_ref[...])
    acc = _dot(h.astype(BF16), wb_ref[...])
    epilogue(acc, h, w_ref, extra, outs)


def _rope_heads(x, cos, sin_hi, sin_lo):
    parts = []
    for h in range(HEADS):
        xh = x[:, h * DH:(h + 1) * DH]
        parts.append(xh * cos + pltpu.roll(xh, DH - ROT_DIM // 2, 1) * sin_hi
                     + pltpu.roll(xh, ROT_DIM // 2, 1) * sin_lo)
    return jnp.concatenate(parts, axis=1)


def _epi_attn(acc, h, w_ref, extra, outs, *, out_dtype):
    cos_ref, shi_ref, slo_ref = extra
    q_ref, k_ref, v_ref = outs
    cos, shi, slo = cos_ref[...], shi_ref[...], slo_ref[...]
    q = _rope_heads(acc[:, :WIDTH], cos, shi, slo) * (DH ** -0.5)
    k = _rope_heads(acc[:, WIDTH:2 * WIDTH], cos, shi, slo)
    q_ref[...] = q.astype(out_dtype)
    k_ref[...] = k.astype(out_dtype)
    v_ref[...] = acc[:, 2 * WIDTH:].astype(out_dtype)


def _epi_delta_qkv(acc, h, w_ref, extra, outs, *, with_tail):
    outs[0][...] = acc.astype(outs[0].dtype)
    if with_tail:
        rows = acc.shape[0]
        outs[1][...] = acc[rows - SUBLANES:, :]


def _epi_delta_zab(acc, h, w_ref, extra, outs):
    outs[0][...] = acc.astype(outs[0].dtype)
    outs[1][...] = _dot(h.astype(BF16), extra[0][...].astype(BF16))


def _rope_tables(pos):
    half = ROT_DIM // 2
    inv = ROPE_THETA ** (-jnp.arange(half, dtype=F32) / half)
    ang = pos.astype(F32)[:, None] * inv[None, :]
    cos, sin = jnp.cos(ang), jnp.sin(ang)
    n = pos.shape[0]
    pad = jnp.zeros((n, DH - ROT_DIM), F32)
    z = jnp.zeros((n, half), F32)
    cos_t = jnp.concatenate([cos, cos, jnp.ones((n, DH - ROT_DIM), F32)], axis=1)
    sin_hi = jnp.concatenate([-sin, z, pad], axis=1)
    sin_lo = jnp.concatenate([z, sin, pad], axis=1)
    return cos_t, sin_hi, sin_lo


def _inproj_attn_kernel(x_ref, g_ref, sc_ref, sh_ref, w_ref, cos_ref, shi_ref, slo_ref,
                        q_ref, k_ref, v_ref, wb_ref, hs_ref, hp_ref, *, dil):
    @pl.when(pl.program_id(0) == 0)
    def _():
        _cast_rows(wb_ref, w_ref)

    tm = x_ref.shape[0]
    m = tm // dil
    h = _norm_mod(x_ref[...], g_ref[...], sc_ref[...], sh_ref[...])
    if dil == 1:
        hp = h.astype(BF16)
    else:
        for c in range(h.shape[1] // LANES):
            cs = slice(c * LANES, (c + 1) * LANES)
            hs_ref[c] = h[:, cs]
            for r in range(dil):
                hp_ref[r * m:(r + 1) * m, cs] = hs_ref[c, pl.ds(r, m, stride=dil), :].astype(BF16)
        hp = hp_ref[...]
    acc = _dot(hp, wb_ref[...])
    cos, shi, slo = cos_ref[...], shi_ref[...], slo_ref[...]
    q = (_rope_heads(acc[:, :WIDTH], cos, shi, slo) * (DH ** -0.5)).astype(BF16)
    k = _rope_heads(acc[:, WIDTH:2 * WIDTH], cos, shi, slo).astype(BF16)
    v = acc[:, 2 * WIDTH:].astype(BF16)
    for r in range(dil):
        q_ref[r] = q[r * m:(r + 1) * m]
        k_ref[r] = k[r * m:(r + 1) * m]
        v_ref[r] = v[r * m:(r + 1) * m]


def _inproj_attn_prompt(xall, gain, mod3, w_in, layer, g, dil, tables, b, t):
    d = xall.shape[1]
    tm = ROW_TILE
    tps = t // tm
    ln = t // dil
    m = tm // dil
    tabs = [tb.reshape(tps, m, dil, DH).transpose(0, 2, 1, 3).reshape(t, DH) for tb in tables]
    ncols = 3 * WIDTH
    kernel = functools.partial(_inproj_attn_kernel, dil=dil)
    out_sd = jax.ShapeDtypeStruct((b, dil, ln, WIDTH), BF16)
    out_spec = pl.BlockSpec((None, dil, m, WIDTH), lambda i: (i // tps, 0, i % tps, 0))
    tab_spec = pl.BlockSpec((tm, DH), lambda i: (i % tps, 0))
    return pl.pallas_call(
        kernel,
        grid=(b * tps,),
        in_specs=[pl.BlockSpec((tm, d), lambda i: (i, 0)),
                  pl.BlockSpec((1, d), lambda i: (0, 0)),
                  pl.BlockSpec((None, 1, d), lambda i: (i // tps, 0, 1)),
                  pl.BlockSpec((None, 1, d), lambda i: (i // tps, 0, 0)),
                  pl.BlockSpec((None, d, ncols), lambda i: (layer, 0, g)),
                  tab_spec, tab_spec, tab_spec],
        out_specs=[out_spec, out_spec, out_spec],
        out_shape=[out_sd, out_sd, out_sd],
        scratch_shapes=[pltpu.VMEM((d, ncols), BF16), pltpu.VMEM((d // LANES, tm, LANES), F32),
                        pltpu.VMEM((tm, d), BF16)],
        compiler_params=_params(1),
        name=f"inproj_attn_g{g}",
    )(xall, gain, mod3, mod3, w_in, *tabs)


def _inproj_rows(x2, row0, n_rows, gain, sc_arr, sh_arr, sc_spec, sh_spec, w, wspec, ncols, extra, extra_specs,
                 out_shapes, out_specs, epilogue, tm, name):
    d = x2.shape[1]
    kernel = functools.partial(_inproj_kernel, n_grid=1, n_extra=len(extra), n_out=len(out_shapes),
                               epilogue=epilogue)
    return pl.pallas_call(
        kernel,
        grid=(n_rows // tm,),
        in_specs=[pl.BlockSpec((tm, d), lambda i: (row0 + i, 0)),
                  pl.BlockSpec((1, d), lambda i: (0, 0)),
                  sc_spec, sh_spec, wspec] + list(extra_specs),
        out_specs=out_specs,
        out_shape=out_shapes,
        scratch_shapes=[pltpu.VMEM((d, ncols), BF16)],
        compiler_params=_params(1),
        name=name,
    )(x2, gain, sc_arr, sh_arr, w, *extra)


def _delta_prep_kernel(x_ref, halo_ref, ab_ref, cw_ref, alog_ref, dtb_ref,
                       vb_ref, kh_ref, kl_ref, qd_ref, kdt_ref, aqk_ref, tinv_ref, egl_ref, ext_ref):
    tm = x_ref.shape[0]
    ti = pl.program_id(1)
    halo = halo_ref[...].astype(F32)
    ext_ref[0:SUBLANES, :] = jnp.where(ti > 0, halo, 0.0)
    ext_ref[SUBLANES:, :] = x_ref[...].astype(F32)
    y = ext_ref[pl.ds(SUBLANES - (CONV_W - 1), tm), :] * cw_ref[0:1, :]
    for j in range(1, CONV_W):
        y = y + ext_ref[pl.ds(SUBLANES - (CONV_W - 1) + j, tm), :] * cw_ref[j:j + 1, :]
    y = _silu(y)

    ab = ab_ref[...]
    g = -jnp.exp(alog_ref[...]) * jax.nn.softplus(ab + dtb_ref[...])
    beta = jax.nn.sigmoid(ab)
    row = lax.broadcasted_iota(I32, (tm, tm), 0)
    col = lax.broadcasted_iota(I32, (tm, tm), 1)
    same = (row // CHUNK) == (col // CHUNK)
    tril = jnp.logical_and(same, col <= row)
    stril = jnp.logical_and(same, col < row)
    cum = _dot_exact_lhs(jnp.where(tril, 1.0, 0.0).astype(BF16), g)
    tot = _dot_exact_lhs(jnp.where(same, 1.0, 0.0).astype(BF16), g)
    cum_t = cum.T
    e_cum = jnp.exp(cum)
    e_rest = jnp.exp(tot - cum)
    e_tot = jnp.exp(tot)

    pws = []
    for h in range(HEADS):
        sl = slice(h * DH, (h + 1) * DH)
        qh = y[:, h * DH:(h + 1) * DH]
        kh = y[:, WIDTH + h * DH:WIDTH + (h + 1) * DH]
        vh = y[:, 2 * WIDTH + h * DH:2 * WIDTH + (h + 1) * DH]
        qh = qh * lax.rsqrt(jnp.sum(qh * qh, axis=-1, keepdims=True) + NORM_EPS) * (DH ** -0.5)
        kh = kh * lax.rsqrt(jnp.sum(kh * kh, axis=-1, keepdims=True) + NORM_EPS)
        bcol = beta[:, SUBLANES + h:SUBLANES + h + 1]
        ecum = e_cum[:, h:h + 1]
        qd_ref[:, sl] = (qh * ecum).astype(BF16)
        kdt_ref[h] = (kh * e_rest[:, h:h + 1]).T.astype(BF16)
        kb = kh * bcol
        khb = kh.astype(BF16)
        dec = jnp.exp(jnp.where(tril, cum[:, h:h + 1] - cum_t[h:h + 1, :], -jnp.inf))
        pws.append(jnp.where(stril, _dot_nt(kb.astype(BF16), khb) * dec, 0.0))
        aqk = jnp.where(tril, _dot_nt(qh.astype(BF16), khb) * dec, 0.0)
        for c in range(tm // CHUNK):
            rs = slice(c * CHUNK, (c + 1) * CHUNK)
            aqk_ref[rs, h * CHUNK:(h + 1) * CHUNK] = aqk[rs, rs].astype(BF16)
            egl_ref[c:c + 1, sl] = jnp.broadcast_to(e_tot[c * CHUNK:c * CHUNK + 1, h:h + 1], (1, DH))
        vb_ref[:, sl] = vh * bcol
        k_hi, k_lo = _split2(kb * ecum)
        kh_ref[:, sl] = k_hi
        kl_ref[:, sl] = k_lo

    eye = jnp.where(row == col, 1.0, 0.0)

    def coupling(k):
        return jnp.logical_and(jnp.logical_and((row // (2 * k)) == (col // (2 * k)), (row // k) % 2 == 1),
                               (col // k) % 2 == 0)

    invs = [eye - jnp.where(coupling(1), a, 0.0) for a in pws]
    k = 2
    while k < CHUNK:
        mask = coupling(k)
        mids = [_dot(jnp.where(mask, a, 0.0).astype(BF16), t.astype(BF16)) for a, t in zip(pws, invs)]
        invs = [t - _dot(t.astype(BF16), m.astype(BF16)) for t, m in zip(invs, mids)]
        k *= 2
    for h in range(HEADS):
        for c in range(tm // CHUNK):
            rs = slice(c * CHUNK, (c + 1) * CHUNK)
            tinv_ref[rs, h * CHUNK:(h + 1) * CHUNK] = invs[h][rs, rs].astype(BF16)


def _delta_prep(qkv_pre, ab, conv_w, alog_row, dtb_row, b, t):
    tm = MIX_TILE
    nt = t // tm
    nch = tm // CHUNK
    ncols = 3 * WIDTH
    hb = tm // SUBLANES
    row = lambda bi, ti: (bi * nt + ti, 0)
    return pl.pallas_call(
        _delta_prep_kernel,
        grid=(b, nt),
        in_specs=[pl.BlockSpec((tm, ncols), row),
                  pl.BlockSpec((SUBLANES, ncols), lambda bi, ti: (jnp.maximum((bi * nt + ti) * hb - 1, 0), 0)),
                  pl.BlockSpec((tm, LANES), row),
                  pl.BlockSpec((CONV_W, ncols), lambda bi, ti: (0, 0)),
                  pl.BlockSpec((1, LANES), lambda bi, ti: (0, 0)),
                  pl.BlockSpec((1, LANES), lambda bi, ti: (0, 0))],
        out_specs=[pl.BlockSpec((tm, WIDTH), row),
                   pl.BlockSpec((tm, WIDTH), row),
                   pl.BlockSpec((tm, WIDTH), row),
                   pl.BlockSpec((tm, WIDTH), row),
                   pl.BlockSpec((None, HEADS, DH, tm), lambda bi, ti: (bi * nt + ti, 0, 0, 0)),
                   pl.BlockSpec((tm, HEADS * CHUNK), row),
                   pl.BlockSpec((tm, HEADS * CHUNK), row),
                   pl.BlockSpec((None, nch, WIDTH), lambda bi, ti: (bi * nt + ti, 0, 0))],
        out_shape=[jax.ShapeDtypeStruct((b * t, WIDTH), F32),
                   jax.ShapeDtypeStruct((b * t, WIDTH), BF16),
                   jax.ShapeDtypeStruct((b * t, WIDTH), BF16),
                   jax.ShapeDtypeStruct((b * t, WIDTH), BF16),
                   jax.ShapeDtypeStruct((b * nt, HEADS, DH, tm), BF16),
                   jax.ShapeDtypeStruct((b * t, HEADS * CHUNK), BF16),
                   jax.ShapeDtypeStruct((b * t, HEADS * CHUNK), BF16),
                   jax.ShapeDtypeStruct((b * nt, nch, WIDTH), F32)],
        scratch_shapes=[pltpu.VMEM((tm + SUBLANES, ncols), F32)],
        compiler_params=_params(2),
        name="delta_prep",
    )(qkv_pre, qkv_pre, ab, conv_w, alog_row, dtb_row)


def _delta_scan_kernel(vb_ref, kh_ref, kl_ref, qd_ref, kdt_ref, aqk_ref, tinv_ref, egl_ref,
                       o_ref, s_out_ref, s_ref):
    ti = pl.program_id(1)

    @pl.when(ti == 0)
    def _():
        s_ref[...] = jnp.zeros_like(s_ref)

    tm = vb_ref.shape[0]
    heads = range(HEADS)
    sls = [slice(h * DH, (h + 1) * DH) for h in heads]
    cls = [slice(h * CHUNK, (h + 1) * CHUNK) for h in heads]
    for c in range(tm // CHUNK):
        rs = slice(c * CHUNK, (c + 1) * CHUNK)
        ss = [s_ref[h] for h in heads]
        sp = [_split2(s) for s in ss]
        pred = [_dot(kh_ref[rs, sls[h]], sp[h][0])
                + (_dot(kl_ref[rs, sls[h]], sp[h][0]) + _dot(kh_ref[rs, sls[h]], sp[h][1])) for h in heads]
        qs = [_dot(qd_ref[rs, sls[h]], sp[h][0]) for h in heads]
        res = [(vb_ref[rs, sls[h]] - pred[h]).astype(BF16) for h in heads]
        vbs = [_dot(tinv_ref[rs, cls[h]], res[h]).astype(BF16) for h in heads]
        for h in heads:
            o_ref[rs, sls[h]] = qs[h] + _dot(aqk_ref[rs, cls[h]], vbs[h])
        for h in heads:
            s_ref[h] = ss[h] * egl_ref[c:c + 1, sls[h]] + _dot(kdt_ref[h, :, rs], vbs[h])

    @pl.when(ti == pl.num_programs(1) - 1)
    def _():
        s_out_ref[...] = s_ref[...]


def _delta_scan(vb, kh, kl, qd, kdt, aqk, tinv, egl, b, t):
    tm = MIX_TILE
    nt = t // tm
    nch = tm // CHUNK
    row = lambda bi, ti: (bi * nt + ti, 0)
    return pl.pallas_call(
        _delta_scan_kernel,
        grid=(b, nt),
        in_specs=[pl.BlockSpec((tm, WIDTH), row),
                  pl.BlockSpec((tm, WIDTH), row),
                  pl.BlockSpec((tm, WIDTH), row),
                  pl.BlockSpec((tm, WIDTH), row),
                  pl.BlockSpec((None, HEADS, DH, tm), lambda bi, ti: (bi * nt + ti, 0, 0, 0)),
                  pl.BlockSpec((tm, HEADS * CHUNK), row),
                  pl.BlockSpec((tm, HEADS * CHUNK), row),
                  pl.BlockSpec((None, nch, WIDTH), lambda bi, ti: (bi * nt + ti, 0, 0))],
        out_specs=[pl.BlockSpec((tm, WIDTH), row),
                   pl.BlockSpec((None, HEADS, DH, DH), lambda bi, ti: (bi, 0, 0, 0))],
        out_shape=[jax.ShapeDtypeStruct((b * t, WIDTH), F32),
                   jax.ShapeDtypeStruct((b, HEADS, DH, DH), F32)],
        scratch_shapes=[pltpu.VMEM((HEADS, DH, DH), F32)],
        compiler_params=_params(2),
        name="delta_scan",
    )(vb, kh, kl, qd, kdt, aqk, tinv, egl)


def _delta_sample_kernel(x_ref, buf_ref, ab_ref, cw_ref, alog_ref, dtb_ref, s0_ref,
                         o_ref, s_out_ref, ext_ref, cols_ref, o_scr):
    steps = x_ref.shape[0]
    ext_ref[0:CONV_W - 1, :] = buf_ref[...]
    ext_ref[CONV_W - 1:CONV_W - 1 + steps, :] = x_ref[...]
    y = ext_ref[0:steps, :] * cw_ref[0:1, :]
    for j in range(1, CONV_W):
        y = y + ext_ref[j:j + steps, :] * cw_ref[j:j + 1, :]
    y = _silu(y)
    ab = ab_ref[...]
    eg = jnp.exp(-jnp.exp(alog_ref[...]) * jax.nn.softplus(ab + dtb_ref[...]))
    beta = jax.nn.sigmoid(ab)

    cols_ref[...] = jnp.zeros_like(cols_ref)
    vs = []
    for h in range(HEADS):
        qh = y[:, h * DH:(h + 1) * DH]
        kh = y[:, WIDTH + h * DH:WIDTH + (h + 1) * DH]
        qh = qh * lax.rsqrt(jnp.sum(qh * qh, axis=-1, keepdims=True) + NORM_EPS) * (DH ** -0.5)
        kh = kh * lax.rsqrt(jnp.sum(kh * kh, axis=-1, keepdims=True) + NORM_EPS)
        cols_ref[h * steps:(h + 1) * steps, :] = kh
        cols_ref[(HEADS + h) * steps:(HEADS + h + 1) * steps, :] = qh
        vs.append(y[:, 2 * WIDTH + h * DH:2 * WIDTH + (h + 1) * DH])
    cols = cols_ref[...].T

    for h in range(HEADS):
        s = s0_ref[h]
        for t in range(steps):
            kcol = cols[:, h * steps + t:h * steps + t + 1]
            qcol = cols[:, (HEADS + h) * steps + t:(HEADS + h) * steps + t + 1]
            s = s * eg[t:t + 1, h:h + 1]
            ks = jnp.sum(s * kcol, axis=0, keepdims=True)
            delta = beta[t:t + 1, SUBLANES + h:SUBLANES + h + 1] * (vs[h][t:t + 1, :] - ks)
            s = s + kcol * delta
            o_scr[t:t + 1, h * DH:(h + 1) * DH] = jnp.sum(s * qcol, axis=0, keepdims=True)
        s_out_ref[h] = s
    o_ref[...] = o_scr[0:steps, :]


def _delta_sample(qkv_pre, ab, conv_buf, conv_w, alog_row, dtb_row, state, layer, nb, steps):
    ncols = 3 * WIDTH
    xr = qkv_pre.reshape(nb, steps, ncols)
    abr = ab.reshape(nb, steps, LANES)
    return pl.pallas_call(
        _delta_sample_kernel,
        grid=(nb,),
        in_specs=[pl.BlockSpec((None, steps, ncols), lambda i: (i, 0, 0)),
                  pl.BlockSpec((None, None, CONV_W - 1, ncols), lambda i: (layer, i, 0, 0)),
                  pl.BlockSpec((None, steps, LANES), lambda i: (i, 0, 0)),
                  pl.BlockSpec((CONV_W, ncols), lambda i: (0, 0)),
                  pl.BlockSpec((1, LANES), lambda i: (0, 0)),
                  pl.BlockSpec((1, LANES), lambda i: (0, 0)),
                  pl.BlockSpec((None, None, HEADS, DH, DH), lambda i: (layer, i, 0, 0, 0))],
        out_specs=[pl.BlockSpec((None, steps, WIDTH), lambda i: (i, 0, 0)),
                   pl.BlockSpec((None, HEADS, DH, DH), lambda i: (i, 0, 0, 0))],
        out_shape=[jax.ShapeDtypeStruct((nb, steps, WIDTH), F32),
                   jax.ShapeDtypeStruct((nb, HEADS, DH, DH), F32)],
        scratch_shapes=[pltpu.VMEM((2 * SUBLANES, ncols), F32),
                        pltpu.VMEM((DH, DH), F32),
                        pltpu.VMEM((SUBLANES, WIDTH), F32)],
        compiler_params=_params(1),
        name="delta_sample",
    )(xr, conv_buf, abr, conv_w, alog_row, dtb_row, state)


def _attn_prompt_kernel(q_ref, kc_ref, kp_ref, vc_ref, vp_ref, o_ref, lse_ref):
    lt = q_ref.shape[0]
    li = pl.program_id(2)
    nk = 2 * KEYS_BACK
    row = lax.broadcasted_iota(I32, (KEYS_BACK, nk), 0)
    col = lax.broadcasted_iota(I32, (KEYS_BACK, nk), 1)
    window = jnp.logical_and(col >= row, col <= row + KEYS_BACK)
    first = jnp.logical_and(window, jnp.logical_or(li > 0, col >= KEYS_BACK))
    lane = lax.broadcasted_iota(I32, (KEYS_BACK, LANES), 1)
    heads = range(HEADS)
    sls = [slice(h * DH, (h + 1) * DH) for h in heads]
    for blk in range(lt // KEYS_BACK):
        rs = slice(blk * KEYS_BACK, (blk + 1) * KEYS_BACK)
        if blk == 0:
            ks = [jnp.concatenate([kp_ref[:, sl], kc_ref[rs, sl]], axis=0) for sl in sls]
            vs = [jnp.concatenate([vp_ref[:, sl], vc_ref[rs, sl]], axis=0) for sl in sls]
            mask = first
        else:
            both = slice((blk - 1) * KEYS_BACK, (blk + 1) * KEYS_BACK)
            ks = [kc_ref[both, sl] for sl in sls]
            vs = [vc_ref[both, sl] for sl in sls]
            mask = window
        scores = [jnp.where(mask, _dot_nt(q_ref[rs, sls[h]], ks[h]), -jnp.inf) for h in heads]
        ms = [jnp.max(s, axis=-1, keepdims=True) for s in scores]
        ps = [jnp.exp(s - m) for s, m in zip(scores, ms)]
        dens = [jnp.sum(p, axis=-1, keepdims=True) for p in ps]
        outs = [_dot(p.astype(BF16), v) for p, v in zip(ps, vs)]
        lse_tile = jnp.zeros((KEYS_BACK, LANES), F32)
        for h in heads:
            o_ref[rs, sls[h]] = (outs[h] / dens[h]).astype(o_ref.dtype)
            lse_tile = jnp.where(lane == h, ms[h] + jnp.log(dens[h]), lse_tile)
        lse_ref[rs, :] = lse_tile


def _attn_prompt(q, k, v, g):
    b, dil, ln, _ = q.shape
    lt = min(ROW_TILE, ln)
    nb = lt // KEYS_BACK
    cur = pl.BlockSpec((None, None, lt, WIDTH), lambda bi, r, li: (bi, r, li, 0))
    prev = pl.BlockSpec((None, None, KEYS_BACK, WIDTH),
                        lambda bi, r, li: (bi, r, jnp.maximum(li * nb - 1, 0), 0))
    return pl.pallas_call(
        _attn_prompt_kernel,
        grid=(b, dil, ln // lt),
        in_specs=[cur, cur, prev, cur, prev],
        out_specs=[pl.BlockSpec((None, None, lt, WIDTH), lambda bi, r, li: (bi, r, li, 0)),
                   pl.BlockSpec((None, None, lt, LANES), lambda bi, r, li: (bi, r, li, 0))],
        out_shape=[jax.ShapeDtypeStruct((b, dil, ln, WIDTH), BF16),
                   jax.ShapeDtypeStruct((b, dil, ln, LANES), F32)],
        compiler_params=_params(3),
        name=f"attn_prompt_g{g}",
    )(q, k, k, v, v)


def _lane_sums(x):
    shp = x.shape
    flat = x.reshape(-1, LANES).astype(BF16)
    return _dot(flat, jnp.ones((LANES, LANES), BF16)).reshape(shp)


def _attn_sample_kernel(q_ref, k_ref, v_ref, c0_ref, c1_ref, c2_ref, o_ref):
    steps = q_ref.shape[1]
    outs, lses = [], []

    q0, k0, v0 = q_ref[0], k_ref[0], v_ref[0]
    kc, vc = c0_ref[:, 0], c0_ref[:, 1]
    ridx = lax.broadcasted_iota(I32, kc.shape, 0)
    o_rows, l_rows = [], []
    for s in range(steps):
        sc = jnp.where(ridx >= s, _lane_sums(kc * q0[s:s + 1]), -jnp.inf)
        sn = _lane_sums(k0[0:s + 1] * q0[s:s + 1])
        m = jnp.maximum(jnp.max(sc, axis=0, keepdims=True), jnp.max(sn, axis=0, keepdims=True))
        p = jnp.exp(sc - m)
        pn = jnp.exp(sn - m)
        den = jnp.sum(p, axis=0, keepdims=True) + jnp.sum(pn, axis=0, keepdims=True)
        num = jnp.sum(p * vc, axis=0, keepdims=True) + jnp.sum(pn * v0[0:s + 1], axis=0, keepdims=True)
        o_rows.append(num / den)
        l_rows.append(m + jnp.log(den))
    outs.append(jnp.concatenate(o_rows, axis=0))
    lses.append(jnp.concatenate(l_rows, axis=0))

    for gi, cref in ((1, c1_ref), (2, c2_ref)):
        qg, kg, vg = q_ref[gi], k_ref[gi], v_ref[gi]
        if gi == 1:
            dil = GROUPS[1][1]
            nrow = cref.shape[0] // dil
            kc = cref[:, 0].reshape(nrow, dil, HEADS, DH)[:, :steps]
            vc = cref[:, 1].reshape(nrow, dil, HEADS, DH)[:, :steps]
        else:
            kc, vc = cref[:, :, 0], cref[:, :, 1]
        sc = _lane_sums(kc * qg[None])
        sn = _lane_sums(kg * qg)
        m = jnp.maximum(jnp.max(sc, axis=0), sn)
        p = jnp.exp(sc - m[None])
        pn = jnp.exp(sn - m)
        den = jnp.sum(p, axis=0) + pn
        outs.append((jnp.sum(p * vc, axis=0) + pn * vg) / den)
        lses.append(m + jnp.log(den))

    m = jnp.maximum(jnp.maximum(lses[0], lses[1]), lses[2])
    es = [jnp.exp(l - m) for l in lses]
    den = es[0] + es[1] + es[2]
    o_ref[...] = (es[0] * outs[0] + es[1] * outs[1] + es[2] * outs[2]) / den


def _attn_sample(q, k, v, cache0, cache1, cache2, layer):
    nb, _, steps, _, _ = q.shape
    new = pl.BlockSpec((None, 3, steps, HEADS, DH), lambda i: (i, 0, 0, 0, 0))
    w0, w1 = cache0.shape[2], cache1.shape[2]
    dil2 = GROUPS[2][1]
    c2 = cache2.reshape(cache2.shape[0], nb, cache2.shape[2] // dil2, dil2, 2, HEADS, DH)
    return pl.pallas_call(
        _attn_sample_kernel,
        grid=(nb,),
        in_specs=[new, new, new,
                  pl.BlockSpec((None, None, w0, 2, HEADS, DH), lambda i: (layer, i, 0, 0, 0, 0)),
                  pl.BlockSpec((None, None, w1, 2, HEADS, DH), lambda i: (layer, i, 0, 0, 0, 0)),
                  pl.BlockSpec((None, None, c2.shape[2], steps, 2, HEADS, DH),
                               lambda i: (layer, i, 0, 0, 0, 0, 0))],
        out_specs=pl.BlockSpec((None, steps, HEADS, DH), lambda i: (i, 0, 0, 0)),
        out_shape=jax.ShapeDtypeStruct((nb, steps, HEADS, DH), F32),
        compiler_params=_params(1),
        name="attn_sample",
    )(q, k, v, cache0, cache1, c2)


def _pro_delta(refs, scratch, is_sample, y_ref):
    op_ref, zp_ref, os_ref, zs_ref, og_ref = refs
    og = og_ref[...]
    for h in range(HEADS):
        sl = slice(h * DH, (h + 1) * DH)
        oh = jnp.where(is_sample, os_ref[:, sl], op_ref[:, sl])
        zh = jnp.where(is_sample, zs_ref[:, sl], zp_ref[:, sl]).astype(F32)
        oh = oh * lax.rsqrt(jnp.mean(oh * oh, axis=-1, keepdims=True) + NORM_EPS) * og
        y_ref[:, sl] = (oh * _silu(zh)).astype(BF16)


def _pro_attn(refs, scratch, is_sample, y_ref):
    o_refs, l_refs, ys_ref = refs[0:3], refs[3:6], refs[6]
    outs, lses = [], []
    for g, (_, dil) in enumerate(GROUPS):
        if dil == 1:
            outs.append([o_refs[g][0, :, h * DH:(h + 1) * DH].astype(F32) for h in range(HEADS)])
            lses.append(l_refs[g][0])
            continue
        o_scr, l_scr = scratch[2 * (g - 1)], scratch[2 * (g - 1) + 1]
        m = o_refs[g].shape[1]
        for r in range(dil):
            l_scr[pl.ds(r, m, stride=dil), :] = l_refs[g][r]
            for h in range(HEADS):
                o_scr[h, pl.ds(r, m, stride=dil), :] = o_refs[g][r, :, h * DH:(h + 1) * DH].astype(F32)
        outs.append([o_scr[h] for h in range(HEADS)])
        lses.append(l_scr[...])
    a, b, c = lses
    mx = jnp.maximum(jnp.maximum(a, b), c)
    ea, eb, ec = jnp.exp(a - mx), jnp.exp(b - mx), jnp.exp(c - mx)
    den = ea + eb + ec
    wa, wb, wc = ea / den, eb / den, ec / den
    for h in range(HEADS):
        sl = slice(h * DH, (h + 1) * DH)
        y = wa[:, h:h + 1] * outs[0][h] + wb[:, h:h + 1] * outs[1][h] + wc[:, h:h + 1] * outs[2][h]
        y_ref[:, sl] = jnp.where(is_sample, ys_ref[:, sl], y).astype(BF16)


def _route(logits_t):
    rows = [logits_t[e:e + 1, :] for e in range(N_EXPERTS)]
    mx = rows[0]
    for r in rows[1:]:
        mx = jnp.maximum(mx, r)
    ex = [jnp.exp(r - mx) for r in rows]
    gs = []
    for g in range(N_EXPERT_GROUPS):
        a, b, c, d = ex[4 * g:4 * g + 4]
        gs.append(jnp.maximum(jnp.maximum(jnp.maximum(a + b, a + c), jnp.maximum(a + d, b + c)),
                              jnp.maximum(b + d, c + d)))
    best = jnp.maximum(jnp.maximum(gs[0], gs[1]), jnp.maximum(gs[2], gs[3]))
    taken = jnp.zeros_like(best) > 1.0
    gsel = jnp.zeros_like(best)
    p = [jnp.zeros_like(best) for _ in range(EXPERTS_PER_GROUP)]
    for g in range(N_EXPERT_GROUPS):
        here = jnp.logical_and(jnp.logical_not(taken), gs[g] == best)
        taken = jnp.logical_or(taken, here)
        gsel = jnp.where(here, float(g), gsel)
        for k in range(EXPERTS_PER_GROUP):
            p[k] = jnp.where(here, ex[4 * g + k], p[k])

    def first_max(vals):
        top = jnp.maximum(jnp.maximum(vals[0], vals[1]), jnp.maximum(vals[2], vals[3]))
        found = jnp.zeros_like(top) > 1.0
        idx = jnp.zeros_like(top)
        for k in range(EXPERTS_PER_GROUP):
            here = jnp.logical_and(jnp.logical_not(found), vals[k] == top)
            found = jnp.logical_or(found, here)
            idx = jnp.where(here, float(k), idx)
        return top, idx

    v1, i1 = first_max(p)
    rest = [jnp.where(i1 == float(k), -1.0, p[k]) for k in range(EXPERTS_PER_GROUP)]
    v2, i2 = first_max(rest)
    lo = jnp.minimum(i1, i2)
    hi = jnp.maximum(i1, i2)
    p_lo = jnp.where(i1 < i2, v1, v2)
    p_hi = jnp.where(i1 < i2, v2, v1)
    pair = lo * (7.0 - lo) * 0.5 + hi - lo - 1.0
    visit = jnp.zeros_like(pair)
    for rank, (pidx, _, _, _) in enumerate(VISIT):
        visit = jnp.where(pair == float(pidx), float(rank), visit)
    cls = gsel * float(len(PAIRS)) + visit
    tot = p_lo + p_hi
    return cls, p_lo / tot, p_hi / tot


def _piece_copy(src_ref, src_row, dst_ref, dst_row, sem):
    return pltpu.make_async_copy(src_ref.at[pl.ds(pl.multiple_of(src_row, SUBLANES), SUBLANES), :],
                                 dst_ref.at[pl.ds(pl.multiple_of(dst_row, SUBLANES), SUBLANES), :], sem)


def _wait_pieces(src_ref, dst_ref, sem, n):
    def body(p, carry):
        _piece_copy(src_ref, 0, dst_ref, 0, sem).wait()
        return carry

    lax.fori_loop(0, n, body, 0)


def _post_kernel(*refs, n_pro, n_pro_scratch, prologue, n_prompt_tiles):
    pro = refs[:n_pro]
    pro_scratch = refs[len(refs) - n_pro_scratch:]
    (x_ref, wo_ref, g1q_ref, scq_ref, shq_ref, g1r_ref, scr_ref, shr_ref, nf_ref, rw_ref, rb_ref,
     x1_ref, rt_ref, xsort_ref, off_ref, len_ref, cnt_ref,
     wob_ref, y_ref, srt_ref, zero_ref, cnt_v, cnt_s, run_ref, pend_ref, sems, sem_s) = refs[n_pro:len(refs) - n_pro_scratch]
    i = pl.program_id(0)
    n_steps = pl.num_programs(0)
    tm = x_ref.shape[0]
    is_sample = i >= n_prompt_tiles
    slot = i % 2

    @pl.when(i == 0)
    def _():
        _cast_rows(wob_ref, wo_ref)
        zero_ref[...] = jnp.zeros_like(zero_ref)
        for c in range(CLASS_SLOTS):
            run_ref[c] = 0
        pend_ref[0] = 0
        pend_ref[1] = 0

    prologue(pro, pro_scratch, is_sample, y_ref)
    g1 = jnp.where(is_sample, g1r_ref[...], g1q_ref[...])
    sc = jnp.where(is_sample, scr_ref[...], scq_ref[...])
    sh = jnp.where(is_sample, shr_ref[...], shq_ref[...])
    x1 = x_ref[...] + g1 * _dot(y_ref[...], wob_ref[...])
    x1_ref[...] = x1
    h2 = _norm_mod(x1, nf_ref[...], sc, sh)
    logits = _dot_2x2(h2, rw_ref[...]) + rb_ref[...]
    cls, w_lo, w_hi = _route(logits.T)

    cls_i = cls.astype(I32)
    cid = lax.broadcasted_iota(I32, (LANES, tm), 0)
    oh = jnp.where(cid == cls_i, 1.0, 0.0)
    rr = lax.broadcasted_iota(I32, (tm, tm), 0)
    cc = lax.broadcasted_iota(I32, (tm, tm), 1)
    incl = jnp.where(rr <= cc, 1.0, 0.0).astype(BF16)
    pre = _dot(oh.astype(BF16), incl)
    rank = jnp.sum(oh * pre, axis=0, keepdims=True) - 1.0
    cnt_col = jnp.sum(oh, axis=1, keepdims=True)
    pad_col = jnp.floor((cnt_col + (SUBLANES - 1)) * (1.0 / SUBLANES)) * SUBLANES
    r128 = lax.broadcasted_iota(I32, (LANES, LANES), 0)
    c128 = lax.broadcasted_iota(I32, (LANES, LANES), 1)
    below = jnp.where(c128 < r128, 1.0, 0.0).astype(BF16)
    start_col = _dot(below, jnp.broadcast_to(pad_col, (LANES, LANES)).astype(BF16))[:, 0:1]
    dest = rank + jnp.sum(oh * start_col, axis=0, keepdims=True)

    route8 = jnp.concatenate([cls, w_lo, w_hi, dest, jnp.zeros((LANES - 4, tm), F32)], axis=0)
    rt = route8.T
    rt_ref[...] = rt

    lane = lax.broadcasted_iota(I32, (tm, LANES), 1)
    oh_t = jnp.where(lane == rt[:, 0:1].astype(I32), 1.0, 0.0)
    cnt_row = jnp.sum(oh_t, axis=0, keepdims=True)
    pad_row = jnp.floor((cnt_row + (SUBLANES - 1)) * (1.0 / SUBLANES)) * SUBLANES
    above = jnp.where(r128 < c128, 1.0, 0.0).astype(BF16)
    start_row = _dot(jnp.broadcast_to(pad_row, (SUBLANES, LANES)).astype(BF16), above)[0:1, :]
    cnt_v[0:1, :] = pad_row.astype(I32)
    cnt_v[1:2, :] = start_row.astype(I32)
    to_smem = pltpu.make_async_copy(cnt_v, cnt_s, sem_s)
    to_smem.start()

    dest_i = dest.astype(I32)
    perm = jnp.where(lax.broadcasted_iota(I32, (SORT_ROWS, tm), 0) == dest_i, 1.0, 0.0).astype(BF16)
    r1 = rt.astype(BF16)
    r2 = (rt - r1.astype(F32)).astype(BF16)
    r3 = (rt - r1.astype(F32) - r2.astype(F32)).astype(BF16)
    srt_ref[slot, :, 0:WIDTH] = _dot(perm, h2.astype(BF16))
    srt_ref[slot, :, WIDTH:SORT_COLS] = _dot(perm, r1) + _dot(perm, r2) + _dot(perm, r3)
    to_smem.wait()

    src = srt_ref.at[slot]
    total = 0
    for c in range(N_CLASSES):
        n_rows = cnt_s[0, c]
        src0 = cnt_s[1, c]
        dst0 = run_ref[c]
        dst = xsort_ref.at[c]

        def issue(p, carry, src0=src0, dst0=dst0, dst=dst):
            _piece_copy(src, src0 + p * SUBLANES, dst, dst0 + p * SUBLANES, sems.at[slot]).start()
            return carry

        n_pieces = n_rows // SUBLANES
        lax.fori_loop(0, n_pieces, issue, 0)
        off_ref[i * CLASS_SLOTS + c] = dst0
        len_ref[i * CLASS_SLOTS + c] = n_rows
        run_ref[c] = dst0 + n_rows
        total = total + n_pieces
    for c in range(N_CLASSES, CLASS_SLOTS):
        off_ref[i * CLASS_SLOTS + c] = 0
        len_ref[i * CLASS_SLOTS + c] = 0
    pend_ref[slot] = total

    @pl.when(i > 0)
    def _():
        _wait_pieces(srt_ref.at[1 - slot], xsort_ref.at[0], sems.at[1 - slot], pend_ref[1 - slot])

    @pl.when(i == n_steps - 1)
    def _():
        _wait_pieces(src, xsort_ref.at[0], sems.at[slot], total)
        n_zero = 0
        for c in range(N_CLASSES):
            used = run_ref[c]
            cnt_ref[c] = used
            dst = xsort_ref.at[c]
            n_fill = (((used + FFN_TILE - 1) // FFN_TILE) * FFN_TILE - used) // SUBLANES

            def fill(p, carry, used=used, dst=dst):
                _piece_copy(zero_ref, 0, dst, used + p * SUBLANES, sems.at[slot]).start()
                return carry

            lax.fori_loop(0, n_fill, fill, 0)
            n_zero = n_zero + n_fill
        for c in range(N_CLASSES, CLASS_SLOTS):
            cnt_ref[c] = 0
        _wait_pieces(zero_ref, xsort_ref.at[0], sems.at[slot], n_zero)


def _post(pro_arrays, pro_specs, prologue, pro_scratch, xall, w_out, w_spec, mod_seq, mod_rows, norm_ffn,
          rw_pad, rb_pad, n_prompt_tiles, tiles_per_seq, cap):
    n, d = xall.shape
    tm = ROW_TILE
    n_tiles = n // tm
    n_seq = mod_seq.shape[0]
    kernel = functools.partial(_post_kernel, n_pro=len(pro_arrays), n_pro_scratch=len(pro_scratch),
                               prologue=prologue, n_prompt_tiles=n_prompt_tiles)
    const = lambda i: (0, 0)

    def seq_spec(k):
        return pl.BlockSpec((None, 1, d), lambda i: (jnp.minimum(i // tiles_per_seq, n_seq - 1), 0, k))

    def row_spec(k):
        return pl.BlockSpec((tm, d), lambda i: (jnp.maximum(i - n_prompt_tiles, 0), k))

    smem = pl.BlockSpec(memory_space=pltpu.SMEM)
    return pl.pallas_call(
        kernel,
        grid=(n_tiles,),
        in_specs=list(pro_specs) + [
            pl.BlockSpec((tm, d), lambda i: (i, 0)),
            w_spec,
            seq_spec(2), seq_spec(4), seq_spec(3),
            row_spec(2), row_spec(4), row_spec(3),
            pl.BlockSpec((1, d), const),
            pl.BlockSpec((d, LANES), const),
            pl.BlockSpec((1, LANES), const)],
        out_specs=[pl.BlockSpec((tm, d), lambda i: (i, 0)),
                   pl.BlockSpec((tm, LANES), lambda i: (i, 0)),
                   pl.BlockSpec(memory_space=pl.ANY),
                   smem, smem, smem],
        out_shape=[jax.ShapeDtypeStruct((n, d), F32),
                   jax.ShapeDtypeStruct((n, LANES), F32),
                   jax.ShapeDtypeStruct((N_CLASSES, cap, SORT_COLS), F32),
                   jax.ShapeDtypeStruct((n_tiles * CLASS_SLOTS,), I32),
                   jax.ShapeDtypeStruct((n_tiles * CLASS_SLOTS,), I32),
                   jax.ShapeDtypeStruct((CLASS_SLOTS,), I32)],
        scratch_shapes=[pltpu.VMEM((WIDTH, d), BF16),
                        pltpu.VMEM((tm, WIDTH), BF16),
                        pltpu.VMEM((2, SORT_ROWS, SORT_COLS), F32),
                        pltpu.VMEM((SUBLANES, SORT_COLS), F32),
                        pltpu.VMEM((SUBLANES, LANES), I32),
                        pltpu.SMEM((SUBLANES, LANES), I32),
                        pltpu.SMEM((CLASS_SLOTS,), I32),
                        pltpu.SMEM((2,), I32),
                        pltpu.SemaphoreType.DMA((2,)),
                        pltpu.SemaphoreType.DMA(())] + list(pro_scratch),
        compiler_params=_params(1),
        name="post",
    )(*pro_arrays, xall, w_out, mod_seq, mod_seq, mod_seq, mod_rows, mod_rows, mod_rows,
      norm_ffn, rw_pad, rb_pad)


def _ffn_kernel(tc_ref, tl_ref, ea_ref, eb_ref, swap_ref, ca_ref, cb_ref, nu_ref,
                x_ref, ga_ref, ua_ref, da_ref, gb_ref, ub_ref, db_ref,
                o_ref, wa_gu, wa_d, wb_gu, wb_d):
    j = pl.program_id(0)
    f = ga_ref.shape[1]

    @pl.when(ca_ref[j] == 1)
    def _():
        _cast_rows(wa_gu, ga_ref, 0)
        _cast_rows(wa_gu, ua_ref, f)
        _cast_rows(wa_d, da_ref)

    @pl.when(cb_ref[j] == 1)
    def _():
        _cast_rows(wb_gu, gb_ref, 0)
        _cast_rows(wb_gu, ub_ref, f)
        _cast_rows(wb_d, db_ref)

    @pl.when(j < nu_ref[0])
    def _():
        x = x_ref[:, 0:WIDTH].astype(BF16)
        w_lo = x_ref[:, WIDTH + 1:WIDTH + 2]
        w_hi = x_ref[:, WIDTH + 2:WIDTH + 3]
        swapped = swap_ref[j] == 1
        w_a = jnp.where(swapped, w_hi, w_lo)
        w_b = jnp.where(swapped, w_lo, w_hi)
        ha = _dot(x, wa_gu[...])
        act_a = (_silu(ha[:, :f]) * ha[:, f:] * w_a).astype(BF16)
        hb = _dot(x, wb_gu[...])
        act_b = (_silu(hb[:, :f]) * hb[:, f:] * w_b).astype(BF16)
        o_ref[...] = _dot(act_a, wa_d[...]) + _dot(act_b, wb_d[...])


def _ffn(xsort, w_gate, w_up, w_down, layer, tables, n_ffn_tiles):
    ncls, cap, _ = xsort.shape
    d = w_gate.shape[2]
    f = w_gate.shape[3]
    tm = FFN_TILE
    wa_in = pl.BlockSpec((None, None, d, f), lambda j, tc, tl, ea, eb, sw, ca, cb, nu: (layer, ea[j], 0, 0))
    wa_dn = pl.BlockSpec((None, None, f, d), lambda j, tc, tl, ea, eb, sw, ca, cb, nu: (layer, ea[j], 0, 0))
    wb_in = pl.BlockSpec((None, None, d, f), lambda j, tc, tl, ea, eb, sw, ca, cb, nu: (layer, eb[j], 0, 0))
    wb_dn = pl.BlockSpec((None, None, f, d), lambda j, tc, tl, ea, eb, sw, ca, cb, nu: (layer, eb[j], 0, 0))
    rows = lambda j, tc, tl, ea, eb, sw, ca, cb, nu: (tc[j], tl[j], 0)
    return pl.pallas_call(
        _ffn_kernel,
        grid_spec=pltpu.PrefetchScalarGridSpec(
            num_scalar_prefetch=8,
            grid=(n_ffn_tiles,),
            in_specs=[pl.BlockSpec((None, tm, SORT_COLS), rows),
                      wa_in, wa_in, wa_dn, wb_in, wb_in, wb_dn],
            out_specs=pl.BlockSpec((None, tm, d), rows),
            scratch_shapes=[pltpu.VMEM((d, 2 * f), BF16), pltpu.VMEM((f, d), BF16),
                            pltpu.VMEM((d, 2 * f), BF16), pltpu.VMEM((f, d), BF16)]),
        out_shape=jax.ShapeDtypeStruct((ncls, cap, d), F32),
        compiler_params=_params(1),
        name="moe_ffn",
    )(*tables, xsort, w_gate, w_up, w_down, w_gate, w_up, w_down)


def _ffn_tables(counts, n_ffn_tiles):
    cnt = counts[:N_CLASSES]
    tiles = (cnt + FFN_TILE - 1) // FFN_TILE
    ends = jnp.cumsum(tiles)
    nu = ends[-1]
    jj = jnp.minimum(jnp.arange(n_ffn_tiles, dtype=I32), jnp.maximum(nu - 1, 0))
    tcls = jnp.sum((ends[None, :] <= jj[:, None]).astype(I32), axis=1)
    tcls = jnp.minimum(tcls, N_CLASSES - 1)
    tloc = jj - (ends - tiles)[tcls]
    npair = len(PAIRS)
    vtab = jnp.asarray(VISIT, I32)
    grp = tcls // npair
    vis = tcls % npair
    ea = grp * EXPERTS_PER_GROUP + vtab[vis, 1]
    eb = grp * EXPERTS_PER_GROUP + vtab[vis, 2]
    swap = vtab[vis, 3]
    one = jnp.ones((1,), I32)
    ca = jnp.concatenate([one, (ea[1:] != ea[:-1]).astype(I32)])
    cb = jnp.concatenate([one, (eb[1:] != eb[:-1]).astype(I32)])
    return (tcls.astype(I32), tloc.astype(I32), ea.astype(I32), eb.astype(I32), swap.astype(I32),
            ca, cb, nu.astype(I32).reshape(1))


def _combine_kernel(off_ref, len_ref, y_ref, x_ref, rt_ref, g2q_ref, g2r_ref, gain_ref, o_ref, seg_ref, sem,
                    *, final, n_prompt_tiles):
    i = pl.program_id(0)
    tm = o_ref.shape[0]

    @pl.when(i == 0)
    def _():
        seg_ref[...] = jnp.zeros_like(seg_ref)

    start = 0
    total = 0
    for c in range(N_CLASSES):
        n_rows = len_ref[i * CLASS_SLOTS + c]
        src0 = off_ref[i * CLASS_SLOTS + c]
        src = y_ref.at[c]

        def issue(p, carry, src=src, src0=src0, start=start):
            _piece_copy(src, src0 + p * SUBLANES, seg_ref, start + p * SUBLANES, sem).start()
            return carry

        n_pieces = n_rows // SUBLANES
        lax.fori_loop(0, n_pieces, issue, 0)
        start = start + n_rows
        total = total + n_pieces
    _wait_pieces(y_ref.at[0], seg_ref, sem, total)

    dest = rt_ref[:, 3:4].astype(I32)
    unperm = jnp.where(lax.broadcasted_iota(I32, (tm, SORT_ROWS), 1) == dest, 1.0, 0.0).astype(BF16)
    y = _dot(unperm, seg_ref[...].astype(BF16))
    g2 = jnp.where(i >= n_prompt_tiles, g2r_ref[...], g2q_ref[...])
    x2 = x_ref[...] + g2 * y
    if final:
        ms = jnp.mean(x2 * x2, axis=-1, keepdims=True)
        x2 = x2 * lax.rsqrt(ms + NORM_EPS) * gain_ref[...]
    o_ref[...] = x2


def _combine(y_sorted, off, length, x1, rt, mod_seq, mod_rows, gain, final, n_prompt_tiles, tiles_per_seq):
    n, d = x1.shape
    tm = ROW_TILE
    n_seq = mod_seq.shape[0]
    kernel = functools.partial(_combine_kernel, final=final, n_prompt_tiles=n_prompt_tiles)
    return pl.pallas_call(
        kernel,
        grid_spec=pltpu.PrefetchScalarGridSpec(
            num_scalar_prefetch=2,
            grid=(n // tm,),
            in_specs=[pl.BlockSpec(memory_space=pl.ANY),
                      pl.BlockSpec((tm, d), lambda i, o, l: (i, 0)),
                      pl.BlockSpec((tm, LANES), lambda i, o, l: (i, 0)),
                      pl.BlockSpec((None, 1, d),
                                   lambda i, o, l: (jnp.minimum(i // tiles_per_seq, n_seq - 1), 0, 5)),
                      pl.BlockSpec((tm, d), lambda i, o, l: (jnp.maximum(i - n_prompt_tiles, 0), 5)),
                      pl.BlockSpec((1, d), lambda i, o, l: (0, 0))],
            out_specs=pl.BlockSpec((tm, d), lambda i, o, l: (i, 0)),
            scratch_shapes=[pltpu.VMEM((SORT_ROWS, d), F32), pltpu.SemaphoreType.DMA(())]),
        out_shape=jax.ShapeDtypeStruct((n, d), F32),
        compiler_params=_params(1),
        name="combine",
    )(off, length, y_sorted, x1, rt, mod_seq, mod_rows, gain)


def kernel(x_prompt, x_sample, state_conv, state_delta, cache_kv0, cache_kv1, cache_kv2, c_prompt, c_sample,
           w_ada, b_ada, norm_mix, norm_ffn, norm_final, a_w_in, a_conv, a_log, a_dt_bias, a_out_norm, a_w_out,
           b_w_in, b_w_out, router_w, router_b, exp_w_gate, exp_w_up, exp_w_down):
    b, t, d = x_prompt.shape
    nb, steps, _ = x_sample.shape
    depth = w_ada.shape[0]
    n_p, n_s = b * t, nb * steps
    n_all = n_p + n_s
    assert d == WIDTH and t % ROW_TILE == 0 and n_s % ROW_TILE == 0 and t % MIX_TILE == 0
    assert steps <= CONV_W and cache_kv0.shape[2] == GROUPS[0][0] and cache_kv1.shape[2] == GROUPS[1][0]
    assert cache_kv2.shape[2] == GROUPS[2][0] and t % (GROUPS[2][1] * KEYS_BACK) == 0
    tiles_per_seq = t // ROW_TILE
    n_pt, n_st = n_p // ROW_TILE, n_s // ROW_TILE
    n_tiles = n_pt + n_st
    tm = ROW_TILE

    rows_c = b + nb
    rows_pad = -(-rows_c // SUBLANES) * SUBLANES
    c_all = jnp.concatenate([c_prompt, c_sample, jnp.zeros((rows_pad - rows_c, d), F32)], axis=0)
    mod = _adaln(c_all, w_ada, b_ada)
    mod_p = mod[:, :b].reshape(depth, b, 1, 6 * d)
    mod_s = jnp.repeat(mod[:, b:b + nb], steps, axis=1)

    rw_pad = jnp.pad(router_w, ((0, 0), (0, LANES - N_EXPERTS)))
    rb_pad = jnp.pad(router_b, (0, LANES - N_EXPERTS), constant_values=-1e30).reshape(1, LANES)
    tabs_p = _rope_tables(jnp.arange(t, dtype=I32))
    past = cache_kv2.shape[2]
    tabs_s = _rope_tables(past + (jnp.arange(n_s, dtype=I32) % steps))

    cap = -(-(n_all + n_tiles * (SUBLANES - 1)) // FFN_TILE) * FFN_TILE
    n_ffn_tiles = -(-(n_all + n_tiles * N_CLASSES * (SUBLANES - 1)) // FFN_TILE) + N_CLASSES

    xall = jnp.concatenate([x_prompt.reshape(n_p, d), x_sample.reshape(n_s, d)], axis=0)
    convs_p, deltas_p, convs_s, deltas_s = [], [], [], []
    kvs_p = [[] for _ in GROUPS]
    kvs_s = [[] for _ in GROUPS]

    def prompt_rows(width):
        return pl.BlockSpec((tm, width), lambda r: (jnp.minimum(r, n_pt - 1), 0))

    def sample_rows(width):
        return pl.BlockSpec((tm, width), lambda r: (jnp.maximum(r - n_pt, 0), 0))

    for i in range(depth):
        j = i // N_MIXERS
        gain_mix = norm_mix[i].reshape(1, d)
        gain_ffn = norm_ffn[i].reshape(1, d)
        mp, ms = mod_p[i], mod_s[i]
        sc_p = pl.BlockSpec((None, 1, d), lambda r: (r // tiles_per_seq, 0, 1))
        sh_p = pl.BlockSpec((None, 1, d), lambda r: (r // tiles_per_seq, 0, 0))
        sc_s = pl.BlockSpec((tm, d), lambda r: (r, 1))
        sh_s = pl.BlockSpec((tm, d), lambda r: (r, 0))

        if i % N_MIXERS == 0:
            ncols = 3 * WIDTH
            n_gate = a_w_in.shape[2] - ncols - WIDTH
            w_ab = jnp.pad(a_w_in[j][:, ncols + WIDTH:], ((0, 0), (0, LANES - n_gate)))
            alog_row = jnp.pad(a_log[j], (0, LANES - HEADS)).reshape(1, LANES)
            dtb_row = jnp.pad(a_dt_bias[j], (0, LANES - HEADS)).reshape(1, LANES)
            og_row = a_out_norm[j].reshape(1, DH)
            wq_spec = pl.BlockSpec((None, d, ncols), lambda r: (j, 0, 0))
            wz_spec = pl.BlockSpec((None, d, WIDTH), lambda r: (j, 0, ncols // WIDTH))
            wab_spec = [pl.BlockSpec((d, LANES), lambda r: (0, 0))]

            qkv_p, tail_p = _inproj_rows(
                xall, 0, n_p, gain_mix, mp, mp, sc_p, sh_p, a_w_in, wq_spec, ncols, (), (),
                [jax.ShapeDtypeStruct((n_p, ncols), BF16), jax.ShapeDtypeStruct((b, SUBLANES, ncols), F32)],
                [pl.BlockSpec((tm, ncols), lambda r: (r, 0)),
                 pl.BlockSpec((None, SUBLANES, ncols), lambda r: (r // tiles_per_seq, 0, 0))],
                functools.partial(_epi_delta_qkv, with_tail=True), tm, "inproj_delta_qkv")
            z_p, ab_p = _inproj_rows(
                xall, 0, n_p, gain_mix, mp, mp, sc_p, sh_p, a_w_in, wz_spec, WIDTH, (w_ab,), wab_spec,
                [jax.ShapeDtypeStruct((n_p, WIDTH), BF16), jax.ShapeDtypeStruct((n_p, LANES), F32)],
                [pl.BlockSpec((tm, WIDTH), lambda r: (r, 0)),
                 pl.BlockSpec((tm, LANES), lambda r: (r, 0))],
                _epi_delta_zab, tm, "inproj_delta_zab")
            convs_p.append(tail_p[:, SUBLANES - (CONV_W - 1):, :])
            prep = _delta_prep(qkv_p, ab_p, a_conv[j], alog_row, dtb_row, b, t)
            o_p, s_p = _delta_scan(*prep, b, t)
            deltas_p.append(s_p)

            qkv_s, = _inproj_rows(
                xall, n_pt, n_s, gain_mix, ms, ms, sc_s, sh_s, a_w_in, wq_spec, ncols, (), (),
                [jax.ShapeDtypeStruct((n_s, ncols), F32)],
                [pl.BlockSpec((tm, ncols), lambda r: (r, 0))],
                functools.partial(_epi_delta_qkv, with_tail=False), tm, "inproj_delta_qkv_s")
            z_s, ab_s = _inproj_rows(
                xall, n_pt, n_s, gain_mix, ms, ms, sc_s, sh_s, a_w_in, wz_spec, WIDTH, (w_ab,), wab_spec,
                [jax.ShapeDtypeStruct((n_s, WIDTH), BF16), jax.ShapeDtypeStruct((n_s, LANES), F32)],
                [pl.BlockSpec((tm, WIDTH), lambda r: (r, 0)),
                 pl.BlockSpec((tm, LANES), lambda r: (r, 0))],
                _epi_delta_zab, tm, "inproj_delta_zab_s")
            convs_s.append(qkv_s.reshape(nb, steps, ncols)[:, steps - (CONV_W - 1):, :])
            o_s, s_s = _delta_sample(qkv_s, ab_s, state_conv, a_conv[j], alog_row, dtb_row, state_delta,
                                     j, nb, steps)
            deltas_s.append(s_s)
            pro = ([o_p, z_p, o_s.reshape(n_s, WIDTH), z_s, og_row],
                   [prompt_rows(WIDTH), prompt_rows(WIDTH), sample_rows(WIDTH), sample_rows(WIDTH),
                    pl.BlockSpec((1, DH), lambda r: (0, 0))], _pro_delta, [])
            w_out, w_out_spec = a_w_out, pl.BlockSpec((None, WIDTH, d), lambda r: (j, 0, 0))
        else:
            os_, ls_, o_specs, l_specs, pro_scratch = [], [], [], [], []
            for g, (win, dil) in enumerate(GROUPS):
                q, k, v = _inproj_attn_prompt(xall, gain_mix, mp, b_w_in, j, g, dil, tabs_p, b, t)
                o_g, lse_g = _attn_prompt(q, k, v, g)
                os_.append(o_g)
                ls_.append(lse_g)
                mrow = tm // dil

                def grouped(width, dil=dil, mrow=mrow):
                    return pl.BlockSpec((None, dil, mrow, width),
                                        lambda r: (jnp.minimum(r // tiles_per_seq, b - 1), 0, r % tiles_per_seq, 0))

                o_specs.append(grouped(WIDTH))
                l_specs.append(grouped(LANES))
                if dil > 1:
                    pro_scratch += [pltpu.VMEM((HEADS, tm, DH), F32), pltpu.VMEM((tm, LANES), F32)]
                keep = min(win, t)
                lk = keep // dil
                ln = t // dil

                def tail(a):
                    a = a[:, :, ln - lk:, :]
                    return jnp.transpose(a, (0, 2, 1, 3)).reshape(b, keep, HEADS, DH)

                kvs_p[g].append(jnp.stack([tail(k), tail(v)], axis=2).astype(F32))

            qs, ks, vs = [], [], []
            for g in range(len(GROUPS)):
                outs = _inproj_rows(
                    xall, n_pt, n_s, gain_mix, ms, ms, sc_s, sh_s, b_w_in,
                    pl.BlockSpec((None, d, 3 * WIDTH), lambda r, g=g: (j, 0, g)), 3 * WIDTH,
                    tabs_s, [pl.BlockSpec((tm, DH), lambda r: (r, 0))] * 3,
                    [jax.ShapeDtypeStruct((n_s, WIDTH), F32)] * 3,
                    [pl.BlockSpec((tm, WIDTH), lambda r: (r, 0))] * 3,
                    functools.partial(_epi_attn, out_dtype=F32), tm, f"inproj_attn_s_g{g}")
                qs.append(outs[0].reshape(nb, steps, HEADS, DH))
                ks.append(outs[1].reshape(nb, steps, HEADS, DH))
                vs.append(outs[2].reshape(nb, steps, HEADS, DH))
                kvs_s[g].append(jnp.stack([ks[-1], vs[-1]], axis=2))
            y_attn_s = _attn_sample(jnp.stack(qs, axis=1), jnp.stack(ks, axis=1), jnp.stack(vs, axis=1),
                                    cache_kv0, cache_kv1, cache_kv2, j)
            pro = (os_ + ls_ + [y_attn_s.reshape(n_s, WIDTH)],
                   o_specs + l_specs + [sample_rows(WIDTH)], _pro_attn, pro_scratch)
            w_out, w_out_spec = b_w_out, pl.BlockSpec((None, WIDTH, d), lambda r: (j, 0, 0))

        x1, rt, xsort, off, length, counts = _post(
            pro[0], pro[1], pro[2], pro[3], xall, w_out, w_out_spec, mp, ms, gain_ffn, rw_pad, rb_pad,
            n_pt, tiles_per_seq, cap)
        tables = _ffn_tables(counts, n_ffn_tiles)
        y_sorted = _ffn(xsort, exp_w_gate, exp_w_up, exp_w_down, i, tables, n_ffn_tiles)
        xall = _combine(y_sorted, off, length, x1, rt, mp, ms, norm_final.reshape(1, d), i == depth - 1,
                        n_pt, tiles_per_seq)

    y_p = xall[:n_p].reshape(b, t, d)
    y_s = xall[n_p:].reshape(nb, steps, d)
    return (y_p, y_s, jnp.stack(convs_p), jnp.stack(deltas_p),
            jnp.stack(kvs_p[0]), jnp.stack(kvs_p[1]), jnp.stack(kvs_p[2]),
            jnp.stack(convs_s), jnp.stack(deltas_s),
            jnp.stack(kvs_s[0]), jnp.stack(kvs_s[1]), jnp.stack(kvs_s[2]))
```

```python
import functools

import jax
import jax.numpy as jnp
from jax import lax
from jax.experimental import pallas as pl
from jax.experimental.pallas import tpu as pltpu

F32 = jnp.float32
BF16 = jnp.bfloat16
I32 = jnp.int32

N_MIXERS = 2
HEADS = 8
DH = 128
WIDTH = HEADS * DH
CONV_W = 4
CHUNK = 64
GROUPS = ((128, 1), (512, 4), (2048, 16))
KEYS_BACK = 128
ROT_DIM = DH // 4
ROPE_THETA = 500000.0
N_EXPERTS = 16
N_EXPERT_GROUPS = 4
EXPERTS_PER_GROUP = 4
NORM_EPS = 1e-6

LANES = 128
SUBLANES = 8
VMEM_LIMIT_BYTES = 58 * 1024 * 1024

ROW_TILE = 512
MIX_TILE = 256
FFN_TILE = 256

PAIRS = ((0, 1), (0, 2), (0, 3), (1, 2), (1, 3), (2, 3))
VISIT = ((0, 0, 1, 0), (3, 2, 1, 1), (5, 2, 3, 0), (2, 0, 3, 0), (1, 0, 2, 0), (4, 1, 3, 0))
N_CLASSES = N_EXPERT_GROUPS * len(PAIRS)
CLASS_SLOTS = 32
SORT_ROWS = ROW_TILE + 2 * LANES
SORT_COLS = WIDTH + LANES
assert SORT_ROWS >= ROW_TILE + N_CLASSES * (SUBLANES - 1)


def _params(n_grid):
    return pltpu.CompilerParams(dimension_semantics=("arbitrary",) * n_grid,
                                vmem_limit_bytes=VMEM_LIMIT_BYTES)


def _first_step(n_grid):
    first = pl.program_id(0) == 0
    for a in range(1, n_grid):
        first = jnp.logical_and(first, pl.program_id(a) == 0)
    return first


def _silu(x):
    return x * jax.nn.sigmoid(x)


def _norm_mod(x, gain, sc, sh):
    ms = jnp.mean(x * x, axis=-1, keepdims=True)
    return (x * lax.rsqrt(ms + NORM_EPS) * gain) * (1.0 + sc) + sh


def _dot(a, b):
    return jnp.dot(a, b, preferred_element_type=F32)


def _dot_nt(a, b):
    return lax.dot_general(a, b, (((1,), (1,)), ((), ())), preferred_element_type=F32)


def _split2(a):
    hi = a.astype(BF16)
    return hi, (a - hi.astype(F32)).astype(BF16)


def _split3(a):
    hi = a.astype(BF16)
    r = a - hi.astype(F32)
    mid = r.astype(BF16)
    return hi, mid, (r - mid.astype(F32)).astype(BF16)


def _dot_2x2(a, b):
    a_hi, a_lo = _split2(a)
    b_hi, b_lo = _split2(b)
    return _dot(a_hi, b_hi) + (_dot(a_lo, b_hi) + _dot(a_hi, b_lo))


def _dot_exact_lhs(a_b, b):
    b_hi, b_mid, b_lo = _split3(b)
    return _dot(a_b, b_hi) + (_dot(a_b, b_mid) + _dot(a_b, b_lo))


def _cast_rows(dst_ref, src_ref, col0=0):
    rows, ncols = src_ref.shape
    step = 128 if rows % 128 == 0 else rows

    def body(c, carry):
        r0 = pl.multiple_of(c * step, step)
        dst_ref[pl.ds(r0, step), col0:col0 + ncols] = src_ref[pl.ds(r0, step), :].astype(BF16)
        return carry

    lax.fori_loop(0, rows // step, body, 0)


def _adaln_kernel(c_ref, w_ref, b_ref, o_ref):
    o_ref[...] = _dot_2x2(_silu(c_ref[...]), w_ref[...]) + b_ref[...]


def _adaln(c_all, w_ada, b_ada):
    rows, d = c_all.shape
    depth, _, six_d = w_ada.shape
    tn = 1024
    return pl.pallas_call(
        _adaln_kernel,
        grid=(depth, six_d // tn),
        in_specs=[pl.BlockSpec((rows, d), lambda l, j: (0, 0)),
                  pl.BlockSpec((None, d, tn), lambda l, j: (l, 0, j)),
                  pl.BlockSpec((None, 1, tn), lambda l, j: (l, 0, j))],
        out_specs=pl.BlockSpec((None, rows, tn), lambda l, j: (l, 0, j)),
        out_shape=jax.ShapeDtypeStruct((depth, rows, six_d), F32),
        compiler_params=_params(2),
        name="adaln",
    )(c_all, w_ada, b_ada.reshape(depth, 1, six_d))


def _inproj_kernel(x_ref, g_ref, sc_ref, sh_ref, w_ref, *rest, n_grid, n_extra, n_out, epilogue):
    extra = rest[:n_extra]
    outs = rest[n_extra:n_extra + n_out]
    wb_ref = rest[n_extra + n_out]

    @pl.when(_first_step(n_grid))
    def _():
        _cast_rows(wb_ref, w_ref)

    h = _norm_mod(x_ref[...], g_ref[...], sc_ref[...], sh_ref[...])
    acc = _dot(h.astype(BF16), wb_ref[...])
    epilogue(acc, h, w_ref, extra, outs)


def _rope_heads(x, cos, sin_hi, sin_lo):
    parts = []
    for h in range(HEADS):
        xh = x[:, h * DH:(h + 1) * DH]
        parts.append(xh * cos + pltpu.roll(xh, DH - ROT_DIM // 2, 1) * sin_hi
                     + pltpu.roll(xh, ROT_DIM // 2, 1) * sin_lo)
    return jnp.concatenate(parts, axis=1)


def _epi_attn(acc, h, w_ref, extra, outs, *, out_dtype):
    cos_ref, shi_ref, slo_ref = extra
    q_ref, k_ref, v_ref = outs
    cos, shi, slo = cos_ref[...], shi_ref[...], slo_ref[...]
    q = _rope_heads(acc[:, :WIDTH], cos, shi, slo) * (DH ** -0.5)
    k = _rope_heads(acc[:, WIDTH:2 * WIDTH], cos, shi, slo)
    q_ref[...] = q.astype(out_dtype)
    k_ref[...] = k.astype(out_dtype)
    v_ref[...] = acc[:, 2 * WIDTH:].astype(out_dtype)


def _epi_delta_qkv(acc, h, w_ref, extra, outs, *, with_tail):
    outs[0][...] = acc.astype(outs[0].dtype)
    if with_tail:
        rows = acc.shape[0]
        outs[1][...] = acc[rows - SUBLANES:, :]


def _epi_delta_zab(acc, h, w_ref, extra, outs):
    outs[0][...] = acc.astype(outs[0].dtype)
    outs[1][...] = _dot(h.astype(BF16), extra[0][...].astype(BF16))


def _rope_tables(pos):
    half = ROT_DIM // 2
    inv = ROPE_THETA ** (-jnp.arange(half, dtype=F32) / half)
    ang = pos.astype(F32)[:, None] * inv[None, :]
    cos, sin = jnp.cos(ang), jnp.sin(ang)
    n = pos.shape[0]
    pad = jnp.zeros((n, DH - ROT_DIM), F32)
    z = jnp.zeros((n, half), F32)
    cos_t = jnp.concatenate([cos, cos, jnp.ones((n, DH - ROT_DIM), F32)], axis=1)
    sin_hi = jnp.concatenate([-sin, z, pad], axis=1)
    sin_lo = jnp.concatenate([z, sin, pad], axis=1)
    return cos_t, sin_hi, sin_lo


def _inproj_attn_kernel(x_ref, g_ref, sc_ref, sh_ref, w_ref, cos_ref, shi_ref, slo_ref,
                        q_ref, k_ref, v_ref, wb_ref, hs_ref, hp_ref, *, dil):
    @pl.when(pl.program_id(0) == 0)
    def _():
        _cast_rows(wb_ref, w_ref)

    tm = x_ref.shape[0]
    m = tm // dil
    h = _norm_mod(x_ref[...], g_ref[...], sc_ref[...], sh_ref[...])
    if dil == 1:
        hp = h.astype(BF16)
    else:
        for c in range(h.shape[1] // LANES):
            cs = slice(c * LANES, (c + 1) * LANES)
            hs_ref[c] = h[:, cs]
            for r in range(dil):
                hp_ref[r * m:(r + 1) * m, cs] = hs_ref[c, pl.ds(r, m, stride=dil), :].astype(BF16)
        hp = hp_ref[...]
    acc = _dot(hp, wb_ref[...])
    cos, shi, slo = cos_ref[...], shi_ref[...], slo_ref[...]
    q = (_rope_heads(acc[:, :WIDTH], cos, shi, slo) * (DH ** -0.5)).astype(BF16)
    k = _rope_heads(acc[:, WIDTH:2 * WIDTH], cos, shi, slo).astype(BF16)
    v = acc[:, 2 * WIDTH:].astype(BF16)
    for r in range(dil):
        q_ref[r] = q[r * m:(r + 1) * m]
        k_ref[r] = k[r * m:(r + 1) * m]
        v_ref[r] = v[r * m:(r + 1) * m]


def _inproj_attn_prompt(xall, gain, mod3, w_in, layer, g, dil, tables, b, t):
    d = xall.shape[1]
    tm = ROW_TILE
    tps = t // tm
    ln = t // dil
    m = tm // dil
    tabs = [tb.reshape(tps, m, dil, DH).transpose(0, 2, 1, 3).reshape(t, DH) for tb in tables]
    ncols = 3 * WIDTH
    kernel = functools.partial(_inproj_attn_kernel, dil=dil)
    out_sd = jax.ShapeDtypeStruct((b, dil, ln, WIDTH), BF16)
    out_spec = pl.BlockSpec((None, dil, m, WIDTH), lambda i: (i // tps, 0, i % tps, 0))
    tab_spec = pl.BlockSpec((tm, DH), lambda i: (i % tps, 0))
    return pl.pallas_call(
        kernel,
        grid=(b * tps,),
        in_specs=[pl.BlockSpec((tm, d), lambda i: (i, 0)),
                  pl.BlockSpec((1, d), lambda i: (0, 0)),
                  pl.BlockSpec((None, 1, d), lambda i: (i // tps, 0, 1)),
                  pl.BlockSpec((None, 1, d), lambda i: (i // tps, 0, 0)),
                  pl.BlockSpec((None, d, ncols), lambda i: (layer, 0, g)),
                  tab_spec, tab_spec, tab_spec],
        out_specs=[out_spec, out_spec, out_spec],
        out_shape=[out_sd, out_sd, out_sd],
        scratch_shapes=[pltpu.VMEM((d, ncols), BF16), pltpu.VMEM((d // LANES, tm, LANES), F32),
                        pltpu.VMEM((tm, d), BF16)],
        compiler_params=_params(1),
        name=f"inproj_attn_g{g}",
    )(xall, gain, mod3, mod3, w_in, *tabs)


def _inproj_rows(x2, row0, n_rows, gain, sc_arr, sh_arr, sc_spec, sh_spec, w, wspec, ncols, extra, extra_specs,
                 out_shapes, out_specs, epilogue, tm, name):
    d = x2.shape[1]
    kernel = functools.partial(_inproj_kernel, n_grid=1, n_extra=len(extra), n_out=len(out_shapes),
                               epilogue=epilogue)
    return pl.pallas_call(
        kernel,
        grid=(n_rows // tm,),
        in_specs=[pl.BlockSpec((tm, d), lambda i: (row0 + i, 0)),
                  pl.BlockSpec((1, d), lambda i: (0, 0)),
                  sc_spec, sh_spec, wspec] + list(extra_specs),
        out_specs=out_specs,
        out_shape=out_shapes,
        scratch_shapes=[pltpu.VMEM((d, ncols), BF16)],
        compiler_params=_params(1),
        name=name,
    )(x2, gain, sc_arr, sh_arr, w, *extra)


def _delta_prep_kernel(x_ref, halo_ref, ab_ref, cw_ref, alog_ref, dtb_ref,
                       vb_ref, kh_ref, kl_ref, qd_ref, kdt_ref, aqk_ref, tinv_ref, egl_ref, ext_ref):
    tm = x_ref.shape[0]
    ti = pl.program_id(1)
    halo = halo_ref[...].astype(F32)
    ext_ref[0:SUBLANES, :] = jnp.where(ti > 0, halo, 0.0)
    ext_ref[SUBLANES:, :] = x_ref[...].astype(F32)
    y = ext_ref[pl.ds(SUBLANES - (CONV_W - 1), tm), :] * cw_ref[0:1, :]
    for j in range(1, CONV_W):
        y = y + ext_ref[pl.ds(SUBLANES - (CONV_W - 1) + j, tm), :] * cw_ref[j:j + 1, :]
    y = _silu(y)

    ab = ab_ref[...]
    g = -jnp.exp(alog_ref[...]) * jax.nn.softplus(ab + dtb_ref[...])
    beta = jax.nn.sigmoid(ab)
    row = lax.broadcasted_iota(I32, (tm, tm), 0)
    col = lax.broadcasted_iota(I32, (tm, tm), 1)
    same = (row // CHUNK) == (col // CHUNK)
    tril = jnp.logical_and(same, col <= row)
    stril = jnp.logical_and(same, col < row)
    cum = _dot_exact_lhs(jnp.where(tril, 1.0, 0.0).astype(BF16), g)
    tot = _dot_exact_lhs(jnp.where(same, 1.0, 0.0).astype(BF16), g)
    cum_t = cum.T
    e_cum = jnp.exp(cum)
    e_rest = jnp.exp(tot - cum)
    e_tot = jnp.exp(tot)

    pws = []
    for h in range(HEADS):
        sl = slice(h * DH, (h + 1) * DH)
        qh = y[:, h * DH:(h + 1) * DH]
        kh = y[:, WIDTH + h * DH:WIDTH + (h + 1) * DH]
        vh = y[:, 2 * WIDTH + h * DH:2 * WIDTH + (h + 1) * DH]
        qh = qh * lax.rsqrt(jnp.sum(qh * qh, axis=-1, keepdims=True) + NORM_EPS) * (DH ** -0.5)
        kh = kh * lax.rsqrt(jnp.sum(kh * kh, axis=-1, keepdims=True) + NORM_EPS)
        bcol = beta[:, SUBLANES + h:SUBLANES + h + 1]
        ecum = e_cum[:, h:h + 1]
        qd_ref[:, sl] = (qh * ecum).astype(BF16)
        kdt_ref[h] = (kh * e_rest[:, h:h + 1]).T.astype(BF16)
        kb = kh * bcol
        khb = kh.astype(BF16)
        dec = jnp.exp(jnp.where(tril, cum[:, h:h + 1] - cum_t[h:h + 1, :], -jnp.inf))
        pws.append(jnp.where(stril, _dot_nt(kb.astype(BF16), khb) * dec, 0.0))
        aqk = jnp.where(tril, _dot_nt(qh.astype(BF16), khb) * dec, 0.0)
        for c in range(tm // CHUNK):
            rs = slice(c * CHUNK, (c + 1) * CHUNK)
            aqk_ref[rs, h * CHUNK:(h + 1) * CHUNK] = aqk[rs, rs].astype(BF16)
            egl_ref[c:c + 1, sl] = jnp.broadcast_to(e_tot[c * CHUNK:c * CHUNK + 1, h:h + 1], (1, DH))
        vb_ref[:, sl] = vh * bcol
        k_hi, k_lo = _split2(kb * ecum)
        kh_ref[:, sl] = k_hi
        kl_ref[:, sl] = k_lo

    eye = jnp.where(row == col, 1.0, 0.0)

    def coupling(k):
        return jnp.logical_and(jnp.logical_and((row // (2 * k)) == (col // (2 * k)), (row // k) % 2 == 1),
                               (col // k) % 2 == 0)

    invs = [eye - jnp.where(coupling(1), a, 0.0) for a in pws]
    k = 2
    while k < CHUNK:
        mask = coupling(k)
        mids = [_dot(jnp.where(mask, a, 0.0).astype(BF16), t.astype(BF16)) for a, t in zip(pws, invs)]
        invs = [t - _dot(t.astype(BF16), m.astype(BF16)) for t, m in zip(invs, mids)]
        k *= 2
    for h in range(HEADS):
        for c in range(tm // CHUNK):
            rs = slice(c * CHUNK, (c + 1) * CHUNK)
            tinv_ref[rs, h * CHUNK:(h + 1) * CHUNK] = invs[h][rs, rs].astype(BF16)


def _delta_prep(qkv_pre, ab, conv_w, alog_row, dtb_row, b, t):
    tm = MIX_TILE
    nt = t // tm
    nch = tm // CHUNK
    ncols = 3 * WIDTH
    hb = tm // SUBLANES
    row = lambda bi, ti: (bi * nt + ti, 0)
    return pl.pallas_call(
        _delta_prep_kernel,
        grid=(b, nt),
        in_specs=[pl.BlockSpec((tm, ncols), row),
                  pl.BlockSpec((SUBLANES, ncols), lambda bi, ti: (jnp.maximum((bi * nt + ti) * hb - 1, 0), 0)),
                  pl.BlockSpec((tm, LANES), row),
                  pl.BlockSpec((CONV_W, ncols), lambda bi, ti: (0, 0)),
                  pl.BlockSpec((1, LANES), lambda bi, ti: (0, 0)),
                  pl.BlockSpec((1, LANES), lambda bi, ti: (0, 0))],
        out_specs=[pl.BlockSpec((tm, WIDTH), row),
                   pl.BlockSpec((tm, WIDTH), row),
                   pl.BlockSpec((tm, WIDTH), row),
                   pl.BlockSpec((tm, WIDTH), row),
                   pl.BlockSpec((None, HEADS, DH, tm), lambda bi, ti: (bi * nt + ti, 0, 0, 0)),
                   pl.BlockSpec((tm, HEADS * CHUNK), row),
                   pl.BlockSpec((tm, HEADS * CHUNK), row),
                   pl.BlockSpec((None, nch, WIDTH), lambda bi, ti: (bi * nt + ti, 0, 0))],
        out_shape=[jax.ShapeDtypeStruct((b * t, WIDTH), F32),
                   jax.ShapeDtypeStruct((b * t, WIDTH), BF16),
                   jax.ShapeDtypeStruct((b * t, WIDTH), BF16),
                   jax.ShapeDtypeStruct((b * t, WIDTH), BF16),
                   jax.ShapeDtypeStruct((b * nt, HEADS, DH, tm), BF16),
                   jax.ShapeDtypeStruct((b * t, HEADS * CHUNK), BF16),
                   jax.ShapeDtypeStruct((b * t, HEADS * CHUNK), BF16),
                   jax.ShapeDtypeStruct((b * nt, nch, WIDTH), F32)],
        scratch_shapes=[pltpu.VMEM((tm + SUBLANES, ncols), F32)],
        compiler_params=_params(2),
        name="delta_prep",
    )(qkv_pre, qkv_pre, ab, conv_w, alog_row, dtb_row)


def _delta_scan_kernel(vb_ref, kh_ref, kl_ref, qd_ref, kdt_ref, aqk_ref, tinv_ref, egl_ref,
                       o_ref, s_out_ref, s_ref):
    ti = pl.program_id(1)

    @pl.when(ti == 0)
    def _():
        s_ref[...] = jnp.zeros_like(s_ref)

    tm = vb_ref.shape[0]
    heads = range(HEADS)
    sls = [slice(h * DH, (h + 1) * DH) for h in heads]
    cls = [slice(h * CHUNK, (h + 1) * CHUNK) for h in heads]
    for c in range(tm // CHUNK):
        rs = slice(c * CHUNK, (c + 1) * CHUNK)
        ss = [s_ref[h] for h in heads]
        sp = [_split2(s) for s in ss]
        pred = [_dot(kh_ref[rs, sls[h]], sp[h][0])
                + (_dot(kl_ref[rs, sls[h]], sp[h][0]) + _dot(kh_ref[rs, sls[h]], sp[h][1])) for h in heads]
        qs = [_dot(qd_ref[rs, sls[h]], sp[h][0]) for h in heads]
        res = [(vb_ref[rs, sls[h]] - pred[h]).astype(BF16) for h in heads]
        vbs = [_dot(tinv_ref[rs, cls[h]], res[h]).astype(BF16) for h in heads]
        for h in heads:
            o_ref[rs, sls[h]] = qs[h] + _dot(aqk_ref[rs, cls[h]], vbs[h])
        for h in heads:
            s_ref[h] = ss[h] * egl_ref[c:c + 1, sls[h]] + _dot(kdt_ref[h, :, rs], vbs[h])

    @pl.when(ti == pl.num_programs(1) - 1)
    def _():
        s_out_ref[...] = s_ref[...]


def _delta_scan(vb, kh, kl, qd, kdt, aqk, tinv, egl, b, t):
    tm = MIX_TILE
    nt = t // tm
    nch = tm // CHUNK
    row = lambda bi, ti: (bi * nt + ti, 0)
    return pl.pallas_call(
        _delta_scan_kernel,
        grid=(b, nt),
        in_specs=[pl.BlockSpec((tm, WIDTH), row),
                  pl.BlockSpec((tm, WIDTH), row),
                  pl.BlockSpec((tm, WIDTH), row),
                  pl.BlockSpec((tm, WIDTH), row),
                  pl.BlockSpec((None, HEADS, DH, tm), lambda bi, ti: (bi * nt + ti, 0, 0, 0)),
                  pl.BlockSpec((tm, HEADS * CHUNK), row),
                  pl.BlockSpec((tm, HEADS * CHUNK), row),
                  pl.BlockSpec((None, nch, WIDTH), lambda bi, ti: (bi * nt + ti, 0, 0))],
        out_specs=[pl.BlockSpec((tm, WIDTH), row),
                   pl.BlockSpec((None, HEADS, DH, DH), lambda bi, ti: (bi, 0, 0, 0))],
        out_shape=[jax.ShapeDtypeStruct((b * t, WIDTH), F32),
                   jax.ShapeDtypeStruct((b, HEADS, DH, DH), F32)],
        scratch_shapes=[pltpu.VMEM((HEADS, DH, DH), F32)],
        compiler_params=_params(2),
        name="delta_scan",
    )(vb, kh, kl, qd, kdt, aqk, tinv, egl)


def _delta_sample_kernel(x_ref, buf_ref, ab_ref, cw_ref, alog_ref, dtb_ref, s0_ref,
                         o_ref, s_out_ref, ext_ref, cols_ref, o_scr):
    steps = x_ref.shape[0]
    ext_ref[0:CONV_W - 1, :] = buf_ref[...]
    ext_ref[CONV_W - 1:CONV_W - 1 + steps, :] = x_ref[...]
    y = ext_ref[0:steps, :] * cw_ref[0:1, :]
    for j in range(1, CONV_W):
        y = y + ext_ref[j:j + steps, :] * cw_ref[j:j + 1, :]
    y = _silu(y)
    ab = ab_ref[...]
    eg = jnp.exp(-jnp.exp(alog_ref[...]) * jax.nn.softplus(ab + dtb_ref[...]))
    beta = jax.nn.sigmoid(ab)

    cols_ref[...] = jnp.zeros_like(cols_ref)
    vs = []
    for h in range(HEADS):
        qh = y[:, h * DH:(h + 1) * DH]
        kh = y[:, WIDTH + h * DH:WIDTH + (h + 1) * DH]
        qh = qh * lax.rsqrt(jnp.sum(qh * qh, axis=-1, keepdims=True) + NORM_EPS) * (DH ** -0.5)
        kh = kh * lax.rsqrt(jnp.sum(kh * kh, axis=-1, keepdims=True) + NORM_EPS)
        cols_ref[h * steps:(h + 1) * steps, :] = kh
        cols_ref[(HEADS + h) * steps:(HEADS + h + 1) * steps, :] = qh
        vs.append(y[:, 2 * WIDTH + h * DH:2 * WIDTH + (h + 1) * DH])
    cols = cols_ref[...].T

    for h in range(HEADS):
        s = s0_ref[h]
        for t in range(steps):
            kcol = cols[:, h * steps + t:h * steps + t + 1]
            qcol = cols[:, (HEADS + h) * steps + t:(HEADS + h) * steps + t + 1]
            s = s * eg[t:t + 1, h:h + 1]
            ks = jnp.sum(s * kcol, axis=0, keepdims=True)
            delta = beta[t:t + 1, SUBLANES + h:SUBLANES + h + 1] * (vs[h][t:t + 1, :] - ks)
            s = s + kcol * delta
            o_scr[t:t + 1, h * DH:(h + 1) * DH] = jnp.sum(s * qcol, axis=0, keepdims=True)
        s_out_ref[h] = s
    o_ref[...] = o_scr[0:steps, :]


def _delta_sample(qkv_pre, ab, conv_buf, conv_w, alog_row, dtb_row, state, layer, nb, steps):
    ncols = 3 * WIDTH
    xr = qkv_pre.reshape(nb, steps, ncols)
    abr = ab.reshape(nb, steps, LANES)
    return pl.pallas_call(
        _delta_sample_kernel,
        grid=(nb,),
        in_specs=[pl.BlockSpec((None, steps, ncols), lambda i: (i, 0, 0)),
                  pl.BlockSpec((None, None, CONV_W - 1, ncols), lambda i: (layer, i, 0, 0)),
                  pl.BlockSpec((None, steps, LANES), lambda i: (i, 0, 0)),
                  pl.BlockSpec((CONV_W, ncols), lambda i: (0, 0)),
                  pl.BlockSpec((1, LANES), lambda i: (0, 0)),
                  pl.BlockSpec((1, LANES), lambda i: (0, 0)),
                  pl.BlockSpec((None, None, HEADS, DH, DH), lambda i: (layer, i, 0, 0, 0))],
        out_specs=[pl.BlockSpec((None, steps, WIDTH), lambda i: (i, 0, 0)),
                   pl.BlockSpec((None, HEADS, DH, DH), lambda i: (i, 0, 0, 0))],
        out_shape=[jax.ShapeDtypeStruct((nb, steps, WIDTH), F32),
                   jax.ShapeDtypeStruct((nb, HEADS, DH, DH), F32)],
        scratch_shapes=[pltpu.VMEM((2 * SUBLANES, ncols), F32),
                        pltpu.VMEM((DH, DH), F32),
                        pltpu.VMEM((SUBLANES, WIDTH), F32)],
        compiler_params=_params(1),
        name="delta_sample",
    )(xr, conv_buf, abr, conv_w, alog_row, dtb_row, state)


def _attn_prompt_kernel(q_ref, kc_ref, kp_ref, vc_ref, vp_ref, o_ref, lse_ref):
    lt = q_ref.shape[0]
    li = pl.program_id(2)
    nk = 2 * KEYS_BACK
    row = lax.broadcasted_iota(I32, (KEYS_BACK, nk), 0)
    col = lax.broadcasted_iota(I32, (KEYS_BACK, nk), 1)
    window = jnp.logical_and(col >= row, col <= row + KEYS_BACK)
    first = jnp.logical_and(window, jnp.logical_or(li > 0, col >= KEYS_BACK))
    lane = lax.broadcasted_iota(I32, (KEYS_BACK, LANES), 1)
    heads = range(HEADS)
    sls = [slice(h * DH, (h + 1) * DH) for h in heads]
    for blk in range(lt // KEYS_BACK):
        rs = slice(blk * KEYS_BACK, (blk + 1) * KEYS_BACK)
        if blk == 0:
            ks = [jnp.concatenate([kp_ref[:, sl], kc_ref[rs, sl]], axis=0) for sl in sls]
            vs = [jnp.concatenate([vp_ref[:, sl], vc_ref[rs, sl]], axis=0) for sl in sls]
            mask = first
        else:
            both = slice((blk - 1) * KEYS_BACK, (blk + 1) * KEYS_BACK)
            ks = [kc_ref[both, sl] for sl in sls]
            vs = [vc_ref[both, sl] for sl in sls]
            mask = window
        scores = [jnp.where(mask, _dot_nt(q_ref[rs, sls[h]], ks[h]), -jnp.inf) for h in heads]
        ms = [jnp.max(s, axis=-1, keepdims=True) for s in scores]
        ps = [jnp.exp(s - m) for s, m in zip(scores, ms)]
        dens = [jnp.sum(p, axis=-1, keepdims=True) for p in ps]
        outs = [_dot(p.astype(BF16), v) for p, v in zip(ps, vs)]
        lse_tile = jnp.zeros((KEYS_BACK, LANES), F32)
        for h in heads:
            o_ref[rs, sls[h]] = (outs[h] / dens[h]).astype(o_ref.dtype)
            lse_tile = jnp.where(lane == h, ms[h] + jnp.log(dens[h]), lse_tile)
        lse_ref[rs, :] = lse_tile


def _attn_prompt(q, k, v, g):
    b, dil, ln, _ = q.shape
    lt = min(ROW_TILE, ln)
    nb = lt // KEYS_BACK
    cur = pl.BlockSpec((None, None, lt, WIDTH), lambda bi, r, li: (bi, r, li, 0))
    prev = pl.BlockSpec((None, None, KEYS_BACK, WIDTH),
                        lambda bi, r, li: (bi, r, jnp.maximum(li * nb - 1, 0), 0))
    return pl.pallas_call(
        _attn_prompt_kernel,
        grid=(b, dil, ln // lt),
        in_specs=[cur, cur, prev, cur, prev],
        out_specs=[pl.BlockSpec((None, None, lt, WIDTH), lambda bi, r, li: (bi, r, li, 0)),
                   pl.BlockSpec((None, None, lt, LANES), lambda bi, r, li: (bi, r, li, 0))],
        out_shape=[jax.ShapeDtypeStruct((b, dil, ln, WIDTH), BF16),
                   jax.ShapeDtypeStruct((b, dil, ln, LANES), F32)],
        compiler_params=_params(3),
        name=f"attn_prompt_g{g}",
    )(q, k, k, v, v)


def _lane_sums(x):
    shp = x.shape
    flat = x.reshape(-1, LANES).astype(BF16)
    return _dot(flat, jnp.ones((LANES, LANES), BF16)).reshape(shp)


def _attn_sample_kernel(q_ref, k_ref, v_ref, c0_ref, c1_ref, c2_ref, o_ref):
    steps = q_ref.shape[1]
    outs, lses = [], []

    q0, k0, v0 = q_ref[0], k_ref[0], v_ref[0]
    kc, vc = c0_ref[:, 0], c0_ref[:, 1]
    ridx = lax.broadcasted_iota(I32, kc.shape, 0)
    o_rows, l_rows = [], []
    for s in range(steps):
        sc = jnp.where(ridx >= s, _lane_sums(kc * q0[s:s + 1]), -jnp.inf)
        sn = _lane_sums(k0[0:s + 1] * q0[s:s + 1])
        m = jnp.maximum(jnp.max(sc, axis=0, keepdims=True), jnp.max(sn, axis=0, keepdims=True))
        p = jnp.exp(sc - m)
        pn = jnp.exp(sn - m)
        den = jnp.sum(p, axis=0, keepdims=True) + jnp.sum(pn, axis=0, keepdims=True)
        num = jnp.sum(p * vc, axis=0, keepdims=True) + jnp.sum(pn * v0[0:s + 1], axis=0, keepdims=True)
        o_rows.append(num / den)
        l_rows.append(m + jnp.log(den))
    outs.append(jnp.concatenate(o_rows, axis=0))
    lses.append(jnp.concatenate(l_rows, axis=0))

    for gi, cref in ((1, c1_ref), (2, c2_ref)):
        qg, kg, vg = q_ref[gi], k_ref[gi], v_ref[gi]
        if gi == 1:
            dil = GROUPS[1][1]
            nrow = cref.shape[0] // dil
            kc = cref[:, 0].reshape(nrow, dil, HEADS, DH)[:, :steps]
            vc = cref[:, 1].reshape(nrow, dil, HEADS, DH)[:, :steps]
        else:
            kc, vc = cref[:, :, 0], cref[:, :, 1]
        sc = _lane_sums(kc * qg[None])
        sn = _lane_sums(kg * qg)
        m = jnp.maximum(jnp.max(sc, axis=0), sn)
        p = jnp.exp(sc - m[None])
        pn = jnp.exp(sn - m)
        den = jnp.sum(p, axis=0) + pn
        outs.append((jnp.sum(p * vc, axis=0) + pn * vg) / den)
        lses.append(m + jnp.log(den))

    m = jnp.maximum(jnp.maximum(lses[0], lses[1]), lses[2])
    es = [jnp.exp(l - m) for l in lses]
    den = es[0] + es[1] + es[2]
    o_ref[...] = (es[0] * outs[0] + es[1] * outs[1] + es[2] * outs[2]) / den


def _attn_sample(q, k, v, cache0, cache1, cache2, layer):
    nb, _, steps, _, _ = q.shape
    new = pl.BlockSpec((None, 3, steps, HEADS, DH), lambda i: (i, 0, 0, 0, 0))
    w0, w1 = cache0.shape[2], cache1.shape[2]
    dil2 = GROUPS[2][1]
    c2 = cache2.reshape(cache2.shape[0], nb, cache2.shape[2] // dil2, dil2, 2, HEADS, DH)
    return pl.pallas_call(
        _attn_sample_kernel,
        grid=(nb,),
        in_specs=[new, new, new,
                  pl.BlockSpec((None, None, w0, 2, HEADS, DH), lambda i: (layer, i, 0, 0, 0, 0)),
                  pl.BlockSpec((None, None, w1, 2, HEADS, DH), lambda i: (layer, i, 0, 0, 0, 0)),
                  pl.BlockSpec((None, None, c2.shape[2], steps, 2, HEADS, DH),
                               lambda i: (layer, i, 0, 0, 0, 0, 0))],
        out_specs=pl.BlockSpec((None, steps, HEADS, DH), lambda i: (i, 0, 0, 0)),
        out_shape=jax.ShapeDtypeStruct((nb, steps, HEADS, DH), F32),
        compiler_params=_params(1),
        name="attn_sample",
    )(q, k, v, cache0, cache1, c2)


def _pro_delta(refs, scratch, is_sample, y_ref):
    op_ref, zp_ref, os_ref, zs_ref, og_ref = refs
    og = og_ref[...]
    for h in range(HEADS):
        sl = slice(h * DH, (h + 1) * DH)
        oh = jnp.where(is_sample, os_ref[:, sl], op_ref[:, sl])
        zh = jnp.where(is_sample, zs_ref[:, sl], zp_ref[:, sl]).astype(F32)
        oh = oh * lax.rsqrt(jnp.mean(oh * oh, axis=-1, keepdims=True) + NORM_EPS) * og
        y_ref[:, sl] = (oh * _silu(zh)).astype(BF16)


def _pro_attn(refs, scratch, is_sample, y_ref):
    o_refs, l_refs, ys_ref = refs[0:3], refs[3:6], refs[6]
    outs, lses = [], []
    for g, (_, dil) in enumerate(GROUPS):
        if dil == 1:
            outs.append([o_refs[g][0, :, h * DH:(h + 1) * DH].astype(F32) for h in range(HEADS)])
            lses.append(l_refs[g][0])
            continue
        o_scr, l_scr = scratch[2 * (g - 1)], scratch[2 * (g - 1) + 1]
        m = o_refs[g].shape[1]
        for r in range(dil):
            l_scr[pl.ds(r, m, stride=dil), :] = l_refs[g][r]
            for h in range(HEADS):
                o_scr[h, pl.ds(r, m, stride=dil), :] = o_refs[g][r, :, h * DH:(h + 1) * DH].astype(F32)
        outs.append([o_scr[h] for h in range(HEADS)])
        lses.append(l_scr[...])
    a, b, c = lses
    mx = jnp.maximum(jnp.maximum(a, b), c)
    ea, eb, ec = jnp.exp(a - mx), jnp.exp(b - mx), jnp.exp(c - mx)
    den = ea + eb + ec
    wa, wb, wc = ea / den, eb / den, ec / den
    for h in range(HEADS):
        sl = slice(h * DH, (h + 1) * DH)
        y = wa[:, h:h + 1] * outs[0][h] + wb[:, h:h + 1] * outs[1][h] + wc[:, h:h + 1] * outs[2][h]
        y_ref[:, sl] = jnp.where(is_sample, ys_ref[:, sl], y).astype(BF16)


def _route(logits_t):
    rows = [logits_t[e:e + 1, :] for e in range(N_EXPERTS)]
    mx = rows[0]
    for r in rows[1:]:
        mx = jnp.maximum(mx, r)
    ex = [jnp.exp(r - mx) for r in rows]
    gs = []
    for g in range(N_EXPERT_GROUPS):
        a, b, c, d = ex[4 * g:4 * g + 4]
        gs.append(jnp.maximum(jnp.maximum(jnp.maximum(a + b, a + c), jnp.maximum(a + d, b + c)),
                              jnp.maximum(b + d, c + d)))
    best = jnp.maximum(jnp.maximum(gs[0], gs[1]), jnp.maximum(gs[2], gs[3]))
    taken = jnp.zeros_like(best) > 1.0
    gsel = jnp.zeros_like(best)
    p = [jnp.zeros_like(best) for _ in range(EXPERTS_PER_GROUP)]
    for g in range(N_EXPERT_GROUPS):
        here = jnp.logical_and(jnp.logical_not(taken), gs[g] == best)
        taken = jnp.logical_or(taken, here)
        gsel = jnp.where(here, float(g), gsel)
        for k in range(EXPERTS_PER_GROUP):
            p[k] = jnp.where(here, ex[4 * g + k], p[k])

    def first_max(vals):
        top = jnp.maximum(jnp.maximum(vals[0], vals[1]), jnp.maximum(vals[2], vals[3]))
        found = jnp.zeros_like(top) > 1.0
        idx = jnp.zeros_like(top)
        for k in range(EXPERTS_PER_GROUP):
            here = jnp.logical_and(jnp.logical_not(found), vals[k] == top)
            found = jnp.logical_or(found, here)
            idx = jnp.where(here, float(k), idx)
        return top, idx

    v1, i1 = first_max(p)
    rest = [jnp.where(i1 == float(k), -1.0, p[k]) for k in range(EXPERTS_PER_GROUP)]
    v2, i2 = first_max(rest)
    lo = jnp.minimum(i1, i2)
    hi = jnp.maximum(i1, i2)
    p_lo = jnp.where(i1 < i2, v1, v2)
    p_hi = jnp.where(i1 < i2, v2, v1)
    pair = lo * (7.0 - lo) * 0.5 + hi - lo - 1.0
    visit = jnp.zeros_like(pair)
    for rank, (pidx, _, _, _) in enumerate(VISIT):
        visit = jnp.where(pair == float(pidx), float(rank), visit)
    cls = gsel * float(len(PAIRS)) + visit
    tot = p_lo + p_hi
    return cls, p_lo / tot, p_hi / tot


def _piece_copy(src_ref, src_row, dst_ref, dst_row, sem):
    return pltpu.make_async_copy(src_ref.at[pl.ds(pl.multiple_of(src_row, SUBLANES), SUBLANES), :],
                                 dst_ref.at[pl.ds(pl.multiple_of(dst_row, SUBLANES), SUBLANES), :], sem)


def _wait_pieces(src_ref, dst_ref, sem, n):
    def body(p, carry):
        _piece_copy(src_ref, 0, dst_ref, 0, sem).wait()
        return carry

    lax.fori_loop(0, n, body, 0)


def _post_kernel(*refs, n_pro, n_pro_scratch, prologue, n_prompt_tiles):
    pro = refs[:n_pro]
    pro_scratch = refs[len(refs) - n_pro_scratch:]
    (x_ref, wo_ref, g1q_ref, scq_ref, shq_ref, g1r_ref, scr_ref, shr_ref, nf_ref, rw_ref, rb_ref,
     x1_ref, rt_ref, xsort_ref, off_ref, len_ref, cnt_ref,
     wob_ref, y_ref, srt_ref, zero_ref, cnt_v, cnt_s, run_ref, pend_ref, sems, sem_s) = refs[n_pro:len(refs) - n_pro_scratch]
    i = pl.program_id(0)
    n_steps = pl.num_programs(0)
    tm = x_ref.shape[0]
    is_sample = i >= n_prompt_tiles
    slot = i % 2

    @pl.when(i == 0)
    def _():
        _cast_rows(wob_ref, wo_ref)
        zero_ref[...] = jnp.zeros_like(zero_ref)
        for c in range(CLASS_SLOTS):
            run_ref[c] = 0
        pend_ref[0] = 0
        pend_ref[1] = 0

    prologue(pro, pro_scratch, is_sample, y_ref)
    g1 = jnp.where(is_sample, g1r_ref[...], g1q_ref[...])
    sc = jnp.where(is_sample, scr_ref[...], scq_ref[...])
    sh = jnp.where(is_sample, shr_ref[...], shq_ref[...])
    x1 = x_ref[...] + g1 * _dot(y_ref[...], wob_ref[...])
    x1_ref[...] = x1
    h2 = _norm_mod(x1, nf_ref[...], sc, sh)
    logits = _dot_2x2(h2, rw_ref[...]) + rb_ref[...]
    cls, w_lo, w_hi = _route(logits.T)

    cls_i = cls.astype(I32)
    cid = lax.broadcasted_iota(I32, (LANES, tm), 0)
    oh = jnp.where(cid == cls_i, 1.0, 0.0)
    rr = lax.broadcasted_iota(I32, (tm, tm), 0)
    cc = lax.broadcasted_iota(I32, (tm, tm), 1)
    incl = jnp.where(rr <= cc, 1.0, 0.0).astype(BF16)
    pre = _dot(oh.astype(BF16), incl)
    rank = jnp.sum(oh * pre, axis=0, keepdims=True) - 1.0
    cnt_col = jnp.sum(oh, axis=1, keepdims=True)
    pad_col = jnp.floor((cnt_col + (SUBLANES - 1)) * (1.0 / SUBLANES)) * SUBLANES
    r128 = lax.broadcasted_iota(I32, (LANES, LANES), 0)
    c128 = lax.broadcasted_iota(I32, (LANES, LANES), 1)
    below = jnp.where(c128 < r128, 1.0, 0.0).astype(BF16)
    start_col = _dot(below, jnp.broadcast_to(pad_col, (LANES, LANES)).astype(BF16))[:, 0:1]
    dest = rank + jnp.sum(oh * start_col, axis=0, keepdims=True)

    route8 = jnp.concatenate([cls, w_lo, w_hi, dest, jnp.zeros((LANES - 4, tm), F32)], axis=0)
    rt = route8.T
    rt_ref[...] = rt

    lane = lax.broadcasted_iota(I32, (tm, LANES), 1)
    oh_t = jnp.where(lane == rt[:, 0:1].astype(I32), 1.0, 0.0)
    cnt_row = jnp.sum(oh_t, axis=0, keepdims=True)
    pad_row = jnp.floor((cnt_row + (SUBLANES - 1)) * (1.0 / SUBLANES)) * SUBLANES
    above = jnp.where(r128 < c128, 1.0, 0.0).astype(BF16)
    start_row = _dot(jnp.broadcast_to(pad_row, (SUBLANES, LANES)).astype(BF16), above)[0:1, :]
    cnt_v[0:1, :] = pad_row.astype(I32)
    cnt_v[1:2, :] = start_row.astype(I32)
    to_smem = pltpu.make_async_copy(cnt_v, cnt_s, sem_s)
    to_smem.start()

    dest_i = dest.astype(I32)
    perm = jnp.where(lax.broadcasted_iota(I32, (SORT_ROWS, tm), 0) == dest_i, 1.0, 0.0).astype(BF16)
    r1 = rt.astype(BF16)
    r2 = (rt - r1.astype(F32)).astype(BF16)
    r3 = (rt - r1.astype(F32) - r2.astype(F32)).astype(BF16)
    srt_ref[slot, :, 0:WIDTH] = _dot(perm, h2.astype(BF16))
    srt_ref[slot, :, WIDTH:SORT_COLS] = _dot(perm, r1) + _dot(perm, r2) + _dot(perm, r3)
    to_smem.wait()

    src = srt_ref.at[slot]
    total = 0
    for c in range(N_CLASSES):
        n_rows = cnt_s[0, c]
        src0 = cnt_s[1, c]
        dst0 = run_ref[c]
        dst = xsort_ref.at[c]

        def issue(p, carry, src0=src0, dst0=dst0, dst=dst, prio=c % 2):
            _piece_copy(src, src0 + p * SUBLANES, dst, dst0 + p * SUBLANES, sems.at[slot]).start(priority=prio)
            return carry

        n_pieces = n_rows // SUBLANES
        lax.fori_loop(0, n_pieces, issue, 0)
        off_ref[i * CLASS_SLOTS + c] = dst0
        len_ref[i * CLASS_SLOTS + c] = n_rows
        run_ref[c] = dst0 + n_rows
        total = total + n_pieces
    for c in range(N_CLASSES, CLASS_SLOTS):
        off_ref[i * CLASS_SLOTS + c] = 0
        len_ref[i * CLASS_SLOTS + c] = 0
    pend_ref[slot] = total

    @pl.when(i > 0)
    def _():
        _wait_pieces(srt_ref.at[1 - slot], xsort_ref.at[0], sems.at[1 - slot], pend_ref[1 - slot])

    @pl.when(i == n_steps - 1)
    def _():
        _wait_pieces(src, xsort_ref.at[0], sems.at[slot], total)
        n_zero = 0
        for c in range(N_CLASSES):
            used = run_ref[c]
            cnt_ref[c] = used
            dst = xsort_ref.at[c]
            n_fill = (((used + FFN_TILE - 1) // FFN_TILE) * FFN_TILE - used) // SUBLANES

            def fill(p, carry, used=used, dst=dst):
                _piece_copy(zero_ref, 0, dst, used + p * SUBLANES, sems.at[slot]).start()
                return carry

            lax.fori_loop(0, n_fill, fill, 0)
            n_zero = n_zero + n_fill
        for c in range(N_CLASSES, CLASS_SLOTS):
            cnt_ref[c] = 0
        _wait_pieces(zero_ref, xsort_ref.at[0], sems.at[slot], n_zero)


def _post(pro_arrays, pro_specs, prologue, pro_scratch, xall, w_out, w_spec, mod_seq, mod_rows, norm_ffn,
          rw_pad, rb_pad, n_prompt_tiles, tiles_per_seq, cap):
    n, d = xall.shape
    tm = ROW_TILE
    n_tiles = n // tm
    n_seq = mod_seq.shape[0]
    kernel = functools.partial(_post_kernel, n_pro=len(pro_arrays), n_pro_scratch=len(pro_scratch),
                               prologue=prologue, n_prompt_tiles=n_prompt_tiles)
    const = lambda i: (0, 0)

    def seq_spec(k):
        return pl.BlockSpec((None, 1, d), lambda i: (jnp.minimum(i // tiles_per_seq, n_seq - 1), 0, k))

    def row_spec(k):
        return pl.BlockSpec((tm, d), lambda i: (jnp.maximum(i - n_prompt_tiles, 0), k))

    smem = pl.BlockSpec(memory_space=pltpu.SMEM)
    return pl.pallas_call(
        kernel,
        grid=(n_tiles,),
        in_specs=list(pro_specs) + [
            pl.BlockSpec((tm, d), lambda i: (i, 0)),
            w_spec,
            seq_spec(2), seq_spec(4), seq_spec(3),
            row_spec(2), row_spec(4), row_spec(3),
            pl.BlockSpec((1, d), const),
            pl.BlockSpec((d, LANES), const),
            pl.BlockSpec((1, LANES), const)],
        out_specs=[pl.BlockSpec((tm, d), lambda i: (i, 0)),
                   pl.BlockSpec((tm, LANES), lambda i: (i, 0)),
                   pl.BlockSpec(memory_space=pl.ANY),
                   smem, smem, smem],
        out_shape=[jax.ShapeDtypeStruct((n, d), F32),
                   jax.ShapeDtypeStruct((n, LANES), F32),
                   jax.ShapeDtypeStruct((N_CLASSES, cap, SORT_COLS), F32),
                   jax.ShapeDtypeStruct((n_tiles * CLASS_SLOTS,), I32),
                   jax.ShapeDtypeStruct((n_tiles * CLASS_SLOTS,), I32),
                   jax.ShapeDtypeStruct((CLASS_SLOTS,), I32)],
        scratch_shapes=[pltpu.VMEM((WIDTH, d), BF16),
                        pltpu.VMEM((tm, WIDTH), BF16),
                        pltpu.VMEM((2, SORT_ROWS, SORT_COLS), F32),
                        pltpu.VMEM((SUBLANES, SORT_COLS), F32),
                        pltpu.VMEM((SUBLANES, LANES), I32),
                        pltpu.SMEM((SUBLANES, LANES), I32),
                        pltpu.SMEM((CLASS_SLOTS,), I32),
                        pltpu.SMEM((2,), I32),
                        pltpu.SemaphoreType.DMA((2,)),
                        pltpu.SemaphoreType.DMA(())] + list(pro_scratch),
        compiler_params=_params(1),
        name="post",
    )(*pro_arrays, xall, w_out, mod_seq, mod_seq, mod_seq, mod_rows, mod_rows, mod_rows,
      norm_ffn, rw_pad, rb_pad)


def _ffn_kernel(tc_ref, tl_ref, ea_ref, eb_ref, swap_ref, ca_ref, cb_ref, nu_ref,
                x_ref, ga_ref, ua_ref, da_ref, gb_ref, ub_ref, db_ref,
                o_ref, wa_gu, wa_d, wb_gu, wb_d):
    j = pl.program_id(0)
    f = ga_ref.shape[1]

    @pl.when(ca_ref[j] == 1)
    def _():
        _cast_rows(wa_gu, ga_ref, 0)
        _cast_rows(wa_gu, ua_ref, f)
        _cast_rows(wa_d, da_ref)

    @pl.when(cb_ref[j] == 1)
    def _():
        _cast_rows(wb_gu, gb_ref, 0)
        _cast_rows(wb_gu, ub_ref, f)
        _cast_rows(wb_d, db_ref)

    @pl.when(j < nu_ref[0])
    def _():
        x = x_ref[:, 0:WIDTH].astype(BF16)
        w_lo = x_ref[:, WIDTH + 1:WIDTH + 2]
        w_hi = x_ref[:, WIDTH + 2:WIDTH + 3]
        swapped = swap_ref[j] == 1
        w_a = jnp.where(swapped, w_hi, w_lo)
        w_b = jnp.where(swapped, w_lo, w_hi)
        ha = _dot(x, wa_gu[...])
        act_a = (_silu(ha[:, :f]) * ha[:, f:] * w_a).astype(BF16)
        hb = _dot(x, wb_gu[...])
        act_b = (_silu(hb[:, :f]) * hb[:, f:] * w_b).astype(BF16)
        o_ref[...] = _dot(act_a, wa_d[...]) + _dot(act_b, wb_d[...])


def _ffn(xsort, w_gate, w_up, w_down, layer, tables, n_ffn_tiles):
    ncls, cap, _ = xsort.shape
    d = w_gate.shape[2]
    f = w_gate.shape[3]
    tm = FFN_TILE
    wa_in = pl.BlockSpec((None, None, d, f), lambda j, tc, tl, ea, eb, sw, ca, cb, nu: (layer, ea[j], 0, 0))
    wa_dn = pl.BlockSpec((None, None, f, d), lambda j, tc, tl, ea, eb, sw, ca, cb, nu: (layer, ea[j], 0, 0))
    wb_in = pl.BlockSpec((None, None, d, f), lambda j, tc, tl, ea, eb, sw, ca, cb, nu: (layer, eb[j], 0, 0))
    wb_dn = pl.BlockSpec((None, None, f, d), lambda j, tc, tl, ea, eb, sw, ca, cb, nu: (layer, eb[j], 0, 0))
    rows = lambda j, tc, tl, ea, eb, sw, ca, cb, nu: (tc[j], tl[j], 0)
    return pl.pallas_call(
        _ffn_kernel,
        grid_spec=pltpu.PrefetchScalarGridSpec(
            num_scalar_prefetch=8,
            grid=(n_ffn_tiles,),
            in_specs=[pl.BlockSpec((None, tm, SORT_COLS), rows),
                      wa_in, wa_in, wa_dn, wb_in, wb_in, wb_dn],
            out_specs=pl.BlockSpec((None, tm, d), rows),
            scratch_shapes=[pltpu.VMEM((d, 2 * f), BF16), pltpu.VMEM((f, d), BF16),
                            pltpu.VMEM((d, 2 * f), BF16), pltpu.VMEM((f, d), BF16)]),
        out_shape=jax.ShapeDtypeStruct((ncls, cap, d), F32),
        compiler_params=_params(1),
        name="moe_ffn",
    )(*tables, xsort, w_gate, w_up, w_down, w_gate, w_up, w_down)


def _ffn_tables(counts, n_ffn_tiles):
    cnt = counts[:N_CLASSES]
    tiles = (cnt + FFN_TILE - 1) // FFN_TILE
    ends = jnp.cumsum(tiles)
    nu = ends[-1]
    jj = jnp.minimum(jnp.arange(n_ffn_tiles, dtype=I32), jnp.maximum(nu - 1, 0))
    tcls = jnp.sum((ends[None, :] <= jj[:, None]).astype(I32), axis=1)
    tcls = jnp.minimum(tcls, N_CLASSES - 1)
    tloc = jj - (ends - tiles)[tcls]
    npair = len(PAIRS)
    vtab = jnp.asarray(VISIT, I32)
    grp = tcls // npair
    vis = tcls % npair
    ea = grp * EXPERTS_PER_GROUP + vtab[vis, 1]
    eb = grp * EXPERTS_PER_GROUP + vtab[vis, 2]
    swap = vtab[vis, 3]
    one = jnp.ones((1,), I32)
    ca = jnp.concatenate([one, (ea[1:] != ea[:-1]).astype(I32)])
    cb = jnp.concatenate([one, (eb[1:] != eb[:-1]).astype(I32)])
    return (tcls.astype(I32), tloc.astype(I32), ea.astype(I32), eb.astype(I32), swap.astype(I32),
            ca, cb, nu.astype(I32).reshape(1))


def _combine_kernel(off_ref, len_ref, y_ref, x_ref, rt_ref, g2q_ref, g2r_ref, gain_ref, o_ref, seg_ref, sem,
                    *, final, n_prompt_tiles):
    i = pl.program_id(0)
    tm = o_ref.shape[0]

    @pl.when(i == 0)
    def _():
        seg_ref[...] = jnp.zeros_like(seg_ref)

    start = 0
    total = 0
    for c in range(N_CLASSES):
        n_rows = len_ref[i * CLASS_SLOTS + c]
        src0 = off_ref[i * CLASS_SLOTS + c]
        src = y_ref.at[c]

        def issue(p, carry, src=src, src0=src0, start=start, prio=c % 2):
            _piece_copy(src, src0 + p * SUBLANES, seg_ref, start + p * SUBLANES, sem).start(priority=prio)
            return carry

        n_pieces = n_rows // SUBLANES
        lax.fori_loop(0, n_pieces, issue, 0)
        start = start + n_rows
        total = total + n_pieces
    _wait_pieces(y_ref.at[0], seg_ref, sem, total)

    dest = rt_ref[:, 3:4].astype(I32)
    unperm = jnp.where(lax.broadcasted_iota(I32, (tm, SORT_ROWS), 1) == dest, 1.0, 0.0).astype(BF16)
    y = _dot(unperm, seg_ref[...].astype(BF16))
    g2 = jnp.where(i >= n_prompt_tiles, g2r_ref[...], g2q_ref[...])
    x2 = x_ref[...] + g2 * y
    if final:
        ms = jnp.mean(x2 * x2, axis=-1, keepdims=True)
        x2 = x2 * lax.rsqrt(ms + NORM_EPS) * gain_ref[...]
    o_ref[...] = x2


def _combine(y_sorted, off, length, x1, rt, mod_seq, mod_rows, gain, final, n_prompt_tiles, tiles_per_seq):
    n, d = x1.shape
    tm = ROW_TILE
    n_seq = mod_seq.shape[0]
    kernel = functools.partial(_combine_kernel, final=final, n_prompt_tiles=n_prompt_tiles)
    return pl.pallas_call(
        kernel,
        grid_spec=pltpu.PrefetchScalarGridSpec(
            num_scalar_prefetch=2,
            grid=(n // tm,),
            in_specs=[pl.BlockSpec(memory_space=pl.ANY),
                      pl.BlockSpec((tm, d), lambda i, o, l: (i, 0)),
                      pl.BlockSpec((tm, LANES), lambda i, o, l: (i, 0)),
                      pl.BlockSpec((None, 1, d),
                                   lambda i, o, l: (jnp.minimum(i // tiles_per_seq, n_seq - 1), 0, 5)),
                      pl.BlockSpec((tm, d), lambda i, o, l: (jnp.maximum(i - n_prompt_tiles, 0), 5)),
                      pl.BlockSpec((1, d), lambda i, o, l: (0, 0))],
            out_specs=pl.BlockSpec((tm, d), lambda i, o, l: (i, 0)),
            scratch_shapes=[pltpu.VMEM((SORT_ROWS, d), F32), pltpu.SemaphoreType.DMA(())]),
        out_shape=jax.ShapeDtypeStruct((n, d), F32),
        compiler_params=_params(1),
        name="combine",
    )(off, length, y_sorted, x1, rt, mod_seq, mod_rows, gain)


def kernel(x_prompt, x_sample, state_conv, state_delta, cache_kv0, cache_kv1, cache_kv2, c_prompt, c_sample,
           w_ada, b_ada, norm_mix, norm_ffn, norm_final, a_w_in, a_conv, a_log, a_dt_bias, a_out_norm, a_w_out,
           b_w_in, b_w_out, router_w, router_b, exp_w_gate, exp_w_up, exp_w_down):
    b, t, d = x_prompt.shape
    nb, steps, _ = x_sample.shape
    depth = w_ada.shape[0]
    n_p, n_s = b * t, nb * steps
    n_all = n_p + n_s
    assert d == WIDTH and t % ROW_TILE == 0 and n_s % ROW_TILE == 0 and t % MIX_TILE == 0
    assert steps <= CONV_W and cache_kv0.shape[2] == GROUPS[0][0] and cache_kv1.shape[2] == GROUPS[1][0]
    assert cache_kv2.shape[2] == GROUPS[2][0] and t % (GROUPS[2][1] * KEYS_BACK) == 0
    tiles_per_seq = t // ROW_TILE
    n_pt, n_st = n_p // ROW_TILE, n_s // ROW_TILE
    n_tiles = n_pt + n_st
    tm = ROW_TILE

    rows_c = b + nb
    rows_pad = -(-rows_c // SUBLANES) * SUBLANES
    c_all = jnp.concatenate([c_prompt, c_sample, jnp.zeros((rows_pad - rows_c, d), F32)], axis=0)
    mod = _adaln(c_all, w_ada, b_ada)
    mod_p = mod[:, :b].reshape(depth, b, 1, 6 * d)
    mod_s = jnp.repeat(mod[:, b:b + nb], steps, axis=1)

    rw_pad = jnp.pad(router_w, ((0, 0), (0, LANES - N_EXPERTS)))
    rb_pad = jnp.pad(router_b, (0, LANES - N_EXPERTS), constant_values=-1e30).reshape(1, LANES)
    tabs_p = _rope_tables(jnp.arange(t, dtype=I32))
    past = cache_kv2.shape[2]
    tabs_s = _rope_tables(past + (jnp.arange(n_s, dtype=I32) % steps))

    cap = -(-(n_all + n_tiles * (SUBLANES - 1)) // FFN_TILE) * FFN_TILE
    n_ffn_tiles = -(-(n_all + n_tiles * N_CLASSES * (SUBLANES - 1)) // FFN_TILE) + N_CLASSES

    xall = jnp.concatenate([x_prompt.reshape(n_p, d), x_sample.reshape(n_s, d)], axis=0)
    convs_p, deltas_p, convs_s, deltas_s = [], [], [], []
    kvs_p = [[] for _ in GROUPS]
    kvs_s = [[] for _ in GROUPS]

    def prompt_rows(width):
        return pl.BlockSpec((tm, width), lambda r: (jnp.minimum(r, n_pt - 1), 0))

    def sample_rows(width):
        return pl.BlockSpec((tm, width), lambda r: (jnp.maximum(r - n_pt, 0), 0))

    for i in range(depth):
        j = i // N_MIXERS
        gain_mix = norm_mix[i].reshape(1, d)
        gain_ffn = norm_ffn[i].reshape(1, d)
        mp, ms = mod_p[i], mod_s[i]
        sc_p = pl.BlockSpec((None, 1, d), lambda r: (r // tiles_per_seq, 0, 1))
        sh_p = pl.BlockSpec((None, 1, d), lambda r: (r // tiles_per_seq, 0, 0))
        sc_s = pl.BlockSpec((tm, d), lambda r: (r, 1))
        sh_s = pl.BlockSpec((tm, d), lambda r: (r, 0))

        if i % N_MIXERS == 0:
            ncols = 3 * WIDTH
            n_gate = a_w_in.shape[2] - ncols - WIDTH
            w_ab = jnp.pad(a_w_in[j][:, ncols + WIDTH:], ((0, 0), (0, LANES - n_gate)))
            alog_row = jnp.pad(a_log[j], (0, LANES - HEADS)).reshape(1, LANES)
            dtb_row = jnp.pad(a_dt_bias[j], (0, LANES - HEADS)).reshape(1, LANES)
            og_row = a_out_norm[j].reshape(1, DH)
            wq_spec = pl.BlockSpec((None, d, ncols), lambda r: (j, 0, 0))
            wz_spec = pl.BlockSpec((None, d, WIDTH), lambda r: (j, 0, ncols // WIDTH))
            wab_spec = [pl.BlockSpec((d, LANES), lambda r: (0, 0))]

            qkv_p, tail_p = _inproj_rows(
                xall, 0, n_p, gain_mix, mp, mp, sc_p, sh_p, a_w_in, wq_spec, ncols, (), (),
                [jax.ShapeDtypeStruct((n_p, ncols), BF16), jax.ShapeDtypeStruct((b, SUBLANES, ncols), F32)],
                [pl.BlockSpec((tm, ncols), lambda r: (r, 0)),
                 pl.BlockSpec((None, SUBLANES, ncols), lambda r: (r // tiles_per_seq, 0, 0))],
                functools.partial(_epi_delta_qkv, with_tail=True), tm, "inproj_delta_qkv")
            z_p, ab_p = _inproj_rows(
                xall, 0, n_p, gain_mix, mp, mp, sc_p, sh_p, a_w_in, wz_spec, WIDTH, (w_ab,), wab_spec,
                [jax.ShapeDtypeStruct((n_p, WIDTH), BF16), jax.ShapeDtypeStruct((n_p, LANES), F32)],
                [pl.BlockSpec((tm, WIDTH), lambda r: (r, 0)),
                 pl.BlockSpec((tm, LANES), lambda r: (r, 0))],
                _epi_delta_zab, tm, "inproj_delta_zab")
            convs_p.append(tail_p[:, SUBLANES - (CONV_W - 1):, :])
            prep = _delta_prep(qkv_p, ab_p, a_conv[j], alog_row, dtb_row, b, t)
            o_p, s_p = _delta_scan(*prep, b, t)
            deltas_p.append(s_p)

            qkv_s, = _inproj_rows(
                xall, n_pt, n_s, gain_mix, ms, ms, sc_s, sh_s, a_w_in, wq_spec, ncols, (), (),
                [jax.ShapeDtypeStruct((n_s, ncols), F32)],
                [pl.BlockSpec((tm, ncols), lambda r: (r, 0))],
                functools.partial(_epi_delta_qkv, with_tail=False), tm, "inproj_delta_qkv_s")
            z_s, ab_s = _inproj_rows(
                xall, n_pt, n_s, gain_mix, ms, ms, sc_s, sh_s, a_w_in, wz_spec, WIDTH, (w_ab,), wab_spec,
                [jax.ShapeDtypeStruct((n_s, WIDTH), BF16), jax.ShapeDtypeStruct((n_s, LANES), F32)],
                [pl.BlockSpec((tm, WIDTH), lambda r: (r, 0)),
                 pl.BlockSpec((tm, LANES), lambda r: (r, 0))],
                _epi_delta_zab, tm, "inproj_delta_zab_s")
            convs_s.append(qkv_s.reshape(nb, steps, ncols)[:, steps - (CONV_W - 1):, :])
            o_s, s_s = _delta_sample(qkv_s, ab_s, state_conv, a_conv[j], alog_row, dtb_row, state_delta,
                                     j, nb, steps)
            deltas_s.append(s_s)
            pro = ([o_p, z_p, o_s.reshape(n_s, WIDTH), z_s, og_row],
                   [prompt_rows(WIDTH), prompt_rows(WIDTH), sample_rows(WIDTH), sample_rows(WIDTH),
                    pl.BlockSpec((1, DH), lambda r: (0, 0))], _pro_delta, [])
            w_out, w_out_spec = a_w_out, pl.BlockSpec((None, WIDTH, d), lambda r: (j, 0, 0))
        else:
            os_, ls_, o_specs, l_specs, pro_scratch = [], [], [], [], []
            for g, (win, dil) in enumerate(GROUPS):
                q, k, v = _inproj_attn_prompt(xall, gain_mix, mp, b_w_in, j, g, dil, tabs_p, b, t)
                o_g, lse_g = _attn_prompt(q, k, v, g)
                os_.append(o_g)
                ls_.append(lse_g)
                mrow = tm // dil

                def grouped(width, dil=dil, mrow=mrow):
                    return pl.BlockSpec((None, dil, mrow, width),
                                        lambda r: (jnp.minimum(r // tiles_per_seq, b - 1), 0, r % tiles_per_seq, 0))

                o_specs.append(grouped(WIDTH))
                l_specs.append(grouped(LANES))
                if dil > 1:
                    pro_scratch += [pltpu.VMEM((HEADS, tm, DH), F32), pltpu.VMEM((tm, LANES), F32)]
                keep = min(win, t)
                lk = keep // dil
                ln = t // dil

                def tail(a):
                    a = a[:, :, ln - lk:, :]
                    return jnp.transpose(a, (0, 2, 1, 3)).reshape(b, keep, HEADS, DH)

                kvs_p[g].append(jnp.stack([tail(k), tail(v)], axis=2).astype(F32))

            qs, ks, vs = [], [], []
            for g in range(len(GROUPS)):
                outs = _inproj_rows(
                    xall, n_pt, n_s, gain_mix, ms, ms, sc_s, sh_s, b_w_in,
                    pl.BlockSpec((None, d, 3 * WIDTH), lambda r, g=g: (j, 0, g)), 3 * WIDTH,
                    tabs_s, [pl.BlockSpec((tm, DH), lambda r: (r, 0))] * 3,
                    [jax.ShapeDtypeStruct((n_s, WIDTH), F32)] * 3,
                    [pl.BlockSpec((tm, WIDTH), lambda r: (r, 0))] * 3,
                    functools.partial(_epi_attn, out_dtype=F32), tm, f"inproj_attn_s_g{g}")
                qs.append(outs[0].reshape(nb, steps, HEADS, DH))
                ks.append(outs[1].reshape(nb, steps, HEADS, DH))
                vs.append(outs[2].reshape(nb, steps, HEADS, DH))
                kvs_s[g].append(jnp.stack([ks[-1], vs[-1]], axis=2))
            y_attn_s = _attn_sample(jnp.stack(qs, axis=1), jnp.stack(ks, axis=1), jnp.stack(vs, axis=1),
                                    cache_kv0, cache_kv1, cache_kv2, j)
            pro = (os_ + ls_ + [y_attn_s.reshape(n_s, WIDTH)],
                   o_specs + l_specs + [sample_rows(WIDTH)], _pro_attn, pro_scratch)
            w_out, w_out_spec = b_w_out, pl.BlockSpec((None, WIDTH, d), lambda r: (j, 0, 0))

        x1, rt, xsort, off, length, counts = _post(
            pro[0], pro[1], pro[2], pro[3], xall, w_out, w_out_spec, mp, ms, gain_ffn, rw_pad, rb_pad,
            n_pt, tiles_per_seq, cap)
        tables = _ffn_tables(counts, n_ffn_tiles)
        y_sorted = _ffn(xsort, exp_w_gate, exp_w_up, exp_w_down, i, tables, n_ffn_tiles)
        xall = _combine(y_sorted, off, length, x1, rt, mp, ms, norm_final.reshape(1, d), i == depth - 1,
                        n_pt, tiles_per_seq)

    y_p = xall[:n_p].reshape(b, t, d)
    y_s = xall[n_p:].reshape(nb, steps, d)
    return (y_p, y_s, jnp.stack(convs_p), jnp.stack(deltas_p),
            jnp.stack(kvs_p[0]), jnp.stack(kvs_p[1]), jnp.stack(kvs_p[2]),
            jnp.stack(convs_s), jnp.stack(deltas_s),
            jnp.stack(kvs_s[0]), jnp.stack(kvs_s[1]), jnp.stack(kvs_s[2]))
```
